```python
import math
import jax, jax.numpy as jnp
from jax import lax
import numpy as np

D_MODEL = 1024
BATCH = 2
SEQ = 8192
DEPTH = 4

N_A = DEPTH // 2
N_B = DEPTH - N_A
ALPHA = (2.0 * DEPTH) ** 0.25
BETA = (8.0 * DEPTH) ** -0.25
EPS = 1e-5
D_FF = ((8 * D_MODEL // 3 + 255) // 256) * 256
FFN_RES = 0.5
D_INNER = 2 * D_MODEL
M_HEAD_DIM = 64
M_HEADS = D_INNER // M_HEAD_DIM
M_GROUPS = 8
D_STATE = 128
D_CONV = 4
CHUNK = 128
CONV_DIM = D_INNER + 2 * M_GROUPS * D_STATE
IN_PROJ_DIM = D_INNER + CONV_DIM + M_HEADS
A_HEAD_DIM = 64
A_HEADS = D_MODEL // A_HEAD_DIM
BRANCHES = ((128, 1), (512, 4), (2048, 16))
N_BRANCH = len(BRANCHES)
A_WIDTH = A_HEADS * A_HEAD_DIM
Q_DIM = N_BRANCH * A_WIDTH
NUM_BUCKETS = 32
MAX_DISTANCE = 2048
NEG = -1e30

kernel_name = 'yoco_mamba2_dilated_attn_macaron_deepnorm'


def layer_norm(x, g, b):
    xf = x.astype(jnp.float32)
    mu = xf.mean(-1, keepdims=True)
    var = jnp.square(xf - mu).mean(-1, keepdims=True)
    return ((xf - mu) * lax.rsqrt(var + EPS) * g + b).astype(x.dtype)


def swiglu(x, w_in, w_out):
    gate, up = jnp.split(x @ w_in, 2, axis=-1)
    return (jax.nn.silu(gate) * up) @ w_out


def causal_depthwise_conv(u, w, b):
    out = lax.conv_general_dilated(u, w[:, None, :].astype(u.dtype), window_strides=(1,),
                                   padding=[(D_CONV - 1, 0)],
                                   dimension_numbers=('NWC', 'WIO', 'NWC'),
                                   feature_group_count=u.shape[-1])
    return out + b


def ssd_chunked(x, dt, A, Bm, Cm):
    Bz, S, H, P = x.shape
    G, N = Bm.shape[2], Bm.shape[3]
    J = H // G
    c, l = S // CHUNK, CHUNK
    x = x.astype(jnp.float32)
    xdt = (x * dt[..., None]).reshape(Bz, c, l, G, J, P)
    a_cum = jnp.cumsum((dt * A).reshape(Bz, c, l, G, J), axis=2)
    Bc = Bm.astype(jnp.float32).reshape(Bz, c, l, G, N)
    Cc = Cm.astype(jnp.float32).reshape(Bz, c, l, G, N)
    seg = a_cum[:, :, :, None] - a_cum[:, :, None, :]
    causal = jnp.tril(jnp.ones((l, l), dtype=bool))[None, None, :, :, None, None]
    decay = jnp.exp(jnp.where(causal, seg, -jnp.inf))
    cb = jnp.einsum('bclgn,bcsgn->bclsg', Cc, Bc)
    y_diag = jnp.einsum('bclsgj,bcsgjp->bclgjp', cb[..., None] * decay, xdt)
    decay_to_end = jnp.exp(a_cum[:, :, -1:] - a_cum)
    states = jnp.einsum('bclgn,bclgj,bclgjp->bcgjpn', Bc, decay_to_end, xdt)
    chunk_decay = jnp.exp(a_cum[:, :, -1])

    def step(h, inp):
        st, dec = inp
        return dec[..., None, None] * h + st, h

    h0 = jnp.zeros((Bz, G, J, P, N), jnp.float32)
    _, prev = lax.scan(step, h0, (jnp.moveaxis(states, 1, 0), jnp.moveaxis(chunk_decay, 1, 0)))
    prev = jnp.moveaxis(prev, 0, 1)
    y_off = jnp.einsum('bclgn,bcgjpn,bclgj->bclgjp', Cc, prev, jnp.exp(a_cum))
    return (y_diag + y_off).reshape(Bz, S, H, P)


def gated_rmsnorm(y, z, w):
    g = (y * jax.nn.silu(z)).astype(jnp.float32)
    g = g.reshape(*g.shape[:-1], M_GROUPS, -1)
    g = g * lax.rsqrt(jnp.mean(jnp.square(g), -1, keepdims=True) + EPS)
    return (g.reshape(y.shape) * w).astype(y.dtype)


def mamba2_mixer(x, w_in, conv_w, conv_b, dt_bias, a_log, d_skip, norm_w, w_out):
    Bz, S, _ = x.shape
    zxbcdt = x @ w_in
    z = zxbcdt[..., :D_INNER]
    xbc = zxbcdt[..., D_INNER:D_INNER + CONV_DIM]
    dt_raw = zxbcdt[..., D_INNER + CONV_DIM:]
    xbc = jax.nn.silu(causal_depthwise_conv(xbc, conv_w, conv_b))
    xs = xbc[..., :D_INNER].reshape(Bz, S, M_HEADS, M_HEAD_DIM)
    Bm = xbc[..., D_INNER:D_INNER + M_GROUPS * D_STATE].reshape(Bz, S, M_GROUPS, D_STATE)
    Cm = xbc[..., D_INNER + M_GROUPS * D_STATE:].reshape(Bz, S, M_GROUPS, D_STATE)
    dt = jax.nn.softplus(dt_raw.astype(jnp.float32) + dt_bias)
    A = -jnp.exp(a_log.astype(jnp.float32))
    y = ssd_chunked(xs, dt, A, Bm, Cm) + d_skip[:, None] * xs.astype(jnp.float32)
    y = y.reshape(Bz, S, D_INNER).astype(x.dtype)
    return gated_rmsnorm(y, z, norm_w) @ w_out


def t5_causal_bucket(dist):
    max_exact = NUM_BUCKETS // 2
    logv = jnp.log(jnp.maximum(dist, 1).astype(jnp.float32) / max_exact) / math.log(MAX_DISTANCE / max_exact)
    large = jnp.minimum(max_exact + (logv * (NUM_BUCKETS - max_exact)).astype(jnp.int32), NUM_BUCKETS - 1)
    return jnp.where(dist < max_exact, dist, large)


def dilated_branch(q, k, v, bias_hd, r, n):
    Bz, S, H, hd = q.shape
    L = S // r
    nb = -(-L // n)
    Lp = nb * n

    def strided(t):
        return t.reshape(Bz, L, r, H, hd).transpose(0, 2, 1, 3, 4).reshape(Bz * r, L, H, hd)

    def key_blocks(t):
        t = jnp.pad(strided(t), ((0, 0), (n, Lp - L), (0, 0), (0, 0))).reshape(Bz * r, nb + 1, n, H, hd)
        return jnp.concatenate([t[:, :-1], t[:, 1:]], axis=2)

    qs = jnp.pad(strided(q), ((0, 0), (0, Lp - L), (0, 0), (0, 0))).reshape(Bz * r, nb, n, H, hd)
    kb, vb = key_blocks(k), key_blocks(v)
    logits = jnp.einsum('zbihd,zbkhd->zbhik', qs, kb, preferred_element_type=jnp.float32) * (hd ** -0.5)
    qi = jnp.arange(n)[:, None]
    kk = jnp.arange(2 * n)[None, :]
    dist = n + qi - kk
    key_pos = (jnp.arange(nb) * n - n)[:, None, None] + kk[None]
    valid = (dist >= 0) & (dist <= n) & (key_pos >= 0)
    bias = bias_hd[:, jnp.clip(dist, 0, n)].astype(jnp.float32)
    logits = jnp.where(valid[None, :, None], logits + bias[None, None], NEG)
    m = logits.max(-1)
    p = jnp.exp(logits - m[..., None])
    s = p.sum(-1)
    o = jnp.einsum('zbhik,zbkhd->zbihd', p, vb.astype(jnp.float32))
    o = o / jnp.swapaxes(s, 2, 3)[..., None]

    def unstrided(t):
        tail = t.shape[3:]
        t = t.reshape(Bz, r, Lp, *tail)[:, :, :L]
        return jnp.moveaxis(t, 1, 2).reshape(Bz, S, *tail)

    return unstrided(o), unstrided(jnp.swapaxes(m, 2, 3)), unstrided(jnp.swapaxes(s, 2, 3))


def dilated_mixture_mixer(x, w_q, w_o, k_br, v_br, biases):
    Bz, S, _ = x.shape
    q_all = (x @ w_q).reshape(Bz, S, N_BRANCH, A_HEADS, A_HEAD_DIM)
    outs, ms, ss = [], [], []
    for g, (window, r) in enumerate(BRANCHES):
        o, m, s = dilated_branch(q_all[:, :, g], k_br[:, :, g], v_br[:, :, g], biases[g], r, window // r)
        outs.append(o); ms.append(m); ss.append(s)
    m_all = jnp.stack(ms)
    w = jnp.stack(ss) * jnp.exp(m_all - m_all.max(0))
    o = jnp.einsum('gbsh,gbshd->bshd', w, jnp.stack(outs)) / w.sum(0)[..., None]
    return o.reshape(Bz, S, A_WIDTH).astype(x.dtype) @ w_o


def setup_inputs(seed: int = 0) -> dict:
    key = jax.random.key(seed)
    ks = jax.random.split(key, 20)
    nrm = lambda k, shape, scale: jax.random.normal(k, shape, jnp.float32) * scale
    x = nrm(ks[0], (BATCH, SEQ, D_MODEL), 1.0)
    ln_g = 1.0 + nrm(ks[1], (DEPTH, 3, D_MODEL), 0.02)
    ln_b = nrm(ks[2], (DEPTH, 3, D_MODEL), 0.02)
    ffn_w_in = nrm(ks[3], (DEPTH, 2, D_MODEL, 2 * D_FF), D_MODEL ** -0.5)
    ffn_w_out = nrm(ks[4], (DEPTH, 2, D_FF, D_MODEL), BETA * D_FF ** -0.5)
    m_in_proj = nrm(ks[5], (N_A, D_MODEL, IN_PROJ_DIM), D_MODEL ** -0.5)
    m_conv_w = nrm(ks[6], (N_A, D_CONV, CONV_DIM), D_CONV ** -0.5)
    m_conv_b = nrm(ks[7], (N_A, CONV_DIM), 0.01)
    dt0 = jnp.exp(jax.random.uniform(ks[8], (N_A, M_HEADS), jnp.float32, math.log(1e-3), math.log(1e-1)))
    m_dt_bias = dt0 + jnp.log(-jnp.expm1(-dt0))
    m_a_log = jnp.log(jax.random.uniform(ks[9], (N_A, M_HEADS), jnp.float32, 1.0, 16.0))
    m_d = 1.0 + nrm(ks[10], (N_A, M_HEADS), 0.1)
    m_norm_w = 1.0 + nrm(ks[11], (N_A, D_INNER), 0.02)
    m_out_proj = nrm(ks[12], (N_A, D_INNER, D_MODEL), BETA * D_INNER ** -0.5)
    a_w_q = nrm(ks[13], (N_B, D_MODEL, Q_DIM), D_MODEL ** -0.5)
    a_w_o = nrm(ks[14], (N_B, A_WIDTH, D_MODEL), BETA * A_WIDTH ** -0.5)
    kv_k = nrm(ks[15], (D_MODEL, Q_DIM), D_MODEL ** -0.5)
    kv_v = nrm(ks[16], (D_MODEL, Q_DIM), BETA * D_MODEL ** -0.5)
    kv_w = jnp.concatenate([kv_k, kv_v], axis=1)
    rel_bias = nrm(ks[17], (NUM_BUCKETS, N_BRANCH * A_HEADS), 0.5)
    return {'x': x, 'ln_g': ln_g, 'ln_b': ln_b, 'ffn_w_in': ffn_w_in, 'ffn_w_out': ffn_w_out,
            'm_in_proj': m_in_proj, 'm_conv_w': m_conv_w, 'm_conv_b': m_conv_b,
            'm_dt_bias': m_dt_bias, 'm_a_log': m_a_log, 'm_d': m_d, 'm_norm_w': m_norm_w,
            'm_out_proj': m_out_proj, 'a_w_q': a_w_q, 'a_w_o': a_w_o, 'kv_w': kv_w,
            'rel_bias': rel_bias}


def reference(x, ln_g, ln_b, ffn_w_in, ffn_w_out, m_in_proj, m_conv_w, m_conv_b, m_dt_bias,
              m_a_log, m_d, m_norm_w, m_out_proj, a_w_q, a_w_o, kv_w, rel_bias):
    Bz, S, _ = x.shape
    k_br = v_br = biases = None
    for layer in range(DEPTH):
        x = layer_norm(ALPHA * x + FFN_RES * swiglu(x, ffn_w_in[layer, 0], ffn_w_out[layer, 0]),
                       ln_g[layer, 0], ln_b[layer, 0])
        if layer < N_A:
            h = mamba2_mixer(x, m_in_proj[layer], m_conv_w[layer], m_conv_b[layer], m_dt_bias[layer],
                             m_a_log[layer], m_d[layer], m_norm_w[layer], m_out_proj[layer])
        else:
            b = layer - N_A
            h = dilated_mixture_mixer(x, a_w_q[b], a_w_o[b], k_br, v_br, biases)
        x = layer_norm(ALPHA * x + h, ln_g[layer, 1], ln_b[layer, 1])
        x = layer_norm(ALPHA * x + FFN_RES * swiglu(x, ffn_w_in[layer, 1], ffn_w_out[layer, 1]),
                       ln_g[layer, 2], ln_b[layer, 2])
        if layer == N_A - 1:
            kv = x @ kv_w
            k_br = kv[..., :Q_DIM].reshape(Bz, S, N_BRANCH, A_HEADS, A_HEAD_DIM)
            v_br = kv[..., Q_DIM:].reshape(Bz, S, N_BRANCH, A_HEADS, A_HEAD_DIM)
            biases = []
            for g, (window, r) in enumerate(BRANCHES):
                n = window // r
                buckets = t5_causal_bucket(jnp.arange(n + 1, dtype=jnp.int32) * r)
                biases.append(rel_bias[buckets][:, g * A_HEADS:(g + 1) * A_HEADS].T)
    return x
```

```python
import functools
import math

import jax
import jax.numpy as jnp
from jax import lax
from jax.experimental import pallas as pl
from jax.experimental.pallas import tpu as pltpu

F32 = jnp.float32
BF16 = jnp.bfloat16

DEPTH = 4
N_SSD_LAYERS = DEPTH // 2
ALPHA = (2.0 * DEPTH) ** 0.25
EPS = 1e-5
FFN_RES = 0.5
HEAD_DIM = 64
SSD_GROUPS = 8
SSD_STATE = 128
SSD_CHUNK = 128
SSD_CONV = 4
ATTN_BRANCHES = ((128, 1), (512, 4), (2048, 16))
NUM_BUCKETS = 32
MAX_DISTANCE = 2048
NEG = -1e30

LANES = 128
SUBLANES = 8
VMEM_LIMIT_BYTES = 56 * 1024 * 1024


def _params(semantics):
    return pltpu.CompilerParams(dimension_semantics=semantics,
                                vmem_limit_bytes=VMEM_LIMIT_BYTES)


def _resident(shape):
    zeros = (0,) * len(shape)
    return pl.BlockSpec(shape, lambda *_: zeros, pipeline_mode=pl.Buffered(1))


def _layer_norm(y, g, b):
    mu = jnp.mean(y, axis=-1, keepdims=True)
    d = y - mu
    var = jnp.mean(d * d, axis=-1, keepdims=True)
    return d * lax.rsqrt(var + EPS) * g + b


def _silu(v):
    return v * jax.nn.sigmoid(v)


def _ffn_kernel(x_ref, win_ref, wout_ref, g_ref, b_ref, o_ref, *, d_ff, ck):
    x = x_ref[...]
    xb = x.astype(BF16)
    acc = jnp.zeros(x.shape, F32)
    for c in range(d_ff // ck):
        gate = jnp.dot(xb, win_ref[:, c * ck:(c + 1) * ck], preferred_element_type=F32)
        up = jnp.dot(xb, win_ref[:, d_ff + c * ck:d_ff + (c + 1) * ck],
                     preferred_element_type=F32)
        act = (_silu(gate) * up).astype(BF16)
        acc = acc + jnp.dot(act, wout_ref[c * ck:(c + 1) * ck, :], preferred_element_type=F32)
    o_ref[...] = _layer_norm(ALPHA * x + FFN_RES * acc, g_ref[...], b_ref[...])


def ffn_ln(x, w_in, w_out, g, b, *, tm=512, ck=256):
    t, d = x.shape
    d_ff = w_out.shape[0]
    return pl.pallas_call(
        functools.partial(_ffn_kernel, d_ff=d_ff, ck=ck),
        out_shape=jax.ShapeDtypeStruct((t, d), F32),
        grid=(t // tm,),
        in_specs=[pl.BlockSpec((tm, d), lambda i: (i, 0)),
                  _resident(w_in.shape), _resident(w_out.shape),
                  _resident((1, d)), _resident((1, d))],
        out_specs=pl.BlockSpec((tm, d), lambda i: (i, 0)),
        compiler_params=_params(("parallel",)),
        name="ffn_ln",
    )(x, w_in, w_out, g.reshape(1, d), b.reshape(1, d))


def _proj_kernel(x_ref, w_ref, o_ref, *, cn, scale):
    xb = x_ref[...].astype(BF16)
    for c in range(w_ref.shape[1] // cn):
        y = jnp.dot(xb, w_ref[:, c * cn:(c + 1) * cn], preferred_element_type=F32)
        o_ref[:, c * cn:(c + 1) * cn] = (y * scale).astype(o_ref.dtype)


def proj(x, w, *, scale=1.0, tm=512, cn=512, out_dtype=BF16):
    t, d = x.shape
    n = w.shape[1]
    return pl.pallas_call(
        functools.partial(_proj_kernel, cn=cn, scale=scale),
        out_shape=jax.ShapeDtypeStruct((t, n), out_dtype),
        grid=(t // tm,),
        in_specs=[pl.BlockSpec((tm, d), lambda i: (i, 0)), _resident(w.shape)],
        out_specs=pl.BlockSpec((tm, n), lambda i: (i, 0)),
        compiler_params=_params(("parallel",)),
        name="proj",
    )(x, w)


def _ssd_in_kernel(x_ref, wz_ref, wxbc_ref, wdt_ref, wdtt_ref, cw_ref, cb_ref, dtb_ref, dtbt_ref,
                   z_ref, xbc_ref, dt_ref, dtt_ref, halo_ref, *, tm, cn):
    i = pl.program_id(1)
    xb = x_ref[...].astype(BF16)
    hist = SUBLANES

    @pl.when(i == 0)
    def _():
        halo_ref[0:hist, :] = jnp.zeros((hist, halo_ref.shape[1]), F32)

    @pl.when(i > 0)
    def _():
        halo_ref[0:hist, :] = halo_ref[tm:tm + hist, :]

    for c in range(wz_ref.shape[1] // cn):
        sl = slice(c * cn, (c + 1) * cn)
        z_ref[:, sl] = jnp.dot(xb, wz_ref[:, sl], preferred_element_type=F32)
    for c in range(wxbc_ref.shape[1] // cn):
        sl = slice(c * cn, (c + 1) * cn)
        halo_ref[hist:hist + tm, sl] = jnp.dot(xb, wxbc_ref[:, sl], preferred_element_type=F32)
        conv = cb_ref[:, sl] + jnp.zeros((tm, cn), F32)
        for k in range(SSD_CONV):
            off = hist - (SSD_CONV - 1) + k
            conv = conv + cw_ref[k:k + 1, sl] * halo_ref[off:off + tm, sl]
        xbc_ref[:, sl] = _silu(conv)
    dt_raw = jnp.dot(xb, wdt_ref[...], preferred_element_type=F32)
    dt_ref[...] = jax.nn.softplus(dt_raw + dtb_ref[...])
    dtt_raw = lax.dot_general(wdtt_ref[...], xb, (((1,), (1,)), ((), ())),
                              preferred_element_type=F32)
    dtt_ref[...] = jax.nn.softplus(dtt_raw + dtbt_ref[...])


def ssd_in(x, w_z, w_xbc, w_dt, conv_w, conv_b, dt_bias, *, tm=256, cn=512):
    bz, s, d = x.shape
    d_inner, conv_dim, heads = w_z.shape[1], w_xbc.shape[1], w_dt.shape[1]
    kern = functools.partial(_ssd_in_kernel, tm=tm, cn=cn)
    return pl.pallas_call(
        kern,
        out_shape=(jax.ShapeDtypeStruct((bz, s, d_inner), F32),
                   jax.ShapeDtypeStruct((bz, s, conv_dim), F32),
                   jax.ShapeDtypeStruct((bz, s, heads), F32),
                   jax.ShapeDtypeStruct((bz, heads, s), F32)),
        grid=(bz, s // tm),
        in_specs=[pl.BlockSpec((None, tm, d), lambda b, i: (b, i, 0)),
                  _resident(w_z.shape), _resident(w_xbc.shape), _resident(w_dt.shape),
                  _resident((heads, d)), _resident(conv_w.shape), _resident((1, conv_dim)),
                  _resident((1, heads)), _resident((heads, 1))],
        out_specs=(pl.BlockSpec((None, tm, d_inner), lambda b, i: (b, i, 0)),
                   pl.BlockSpec((None, tm, conv_dim), lambda b, i: (b, i, 0)),
                   pl.BlockSpec((None, tm, heads), lambda b, i: (b, i, 0)),
                   pl.BlockSpec((None, heads, tm), lambda b, i: (b, 0, i))),
        scratch_shapes=[pltpu.VMEM((tm + SUBLANES, conv_dim), F32)],
        compiler_params=_params(("parallel", "arbitrary")),
        name="ssd_in",
    )(x, w_z, w_xbc, w_dt, w_dt.T, conv_w, conv_b.reshape(1, conv_dim),
      dt_bias.reshape(1, heads), dt_bias.reshape(heads, 1))


def _ssd_scan_kernel(xs_ref, b_ref, c_ref, dt_ref, dtt_ref, alog_ref, alogt_ref, dskip_ref,
                     y_ref, h_ref, *, cps):
    ck = SSD_CHUNK
    hpg = h_ref.shape[2] // HEAD_DIM
    gw = hpg * HEAD_DIM

    @pl.when(pl.program_id(1) == 0)
    def _():
        h_ref[...] = jnp.zeros(h_ref.shape, F32)

    a_row = -jnp.exp(alog_ref[...])
    a_col = -jnp.exp(alogt_ref[...])
    ri = lax.broadcasted_iota(jnp.int32, (ck, ck), 0)
    ci = lax.broadcasted_iota(jnp.int32, (ck, ck), 1)
    causal = ci <= ri
    tril = causal.astype(F32)
    triu = (ri <= ci).astype(F32)
    lane = lax.broadcasted_iota(jnp.int32, (ck, gw), 1)
    lane1 = lax.broadcasted_iota(jnp.int32, (1, gw), 1)

    def pick_head(vals, lanes):
        out = vals[hpg - 1]
        for j in range(hpg - 2, -1, -1):
            out = jnp.where(lanes < (j + 1) * HEAD_DIM, vals[j], out)
        return out

    for q in range(cps):
        rows = slice(q * ck, (q + 1) * ck)
        dt = dt_ref[rows, :]
        dtt = dtt_ref[:, rows]
        acum = jnp.dot(tril, dt * a_row, precision=lax.Precision.HIGHEST,
                       preferred_element_type=F32)
        acumt = jnp.dot(dtt * a_col, triu, precision=lax.Precision.HIGHEST,
                        preferred_element_type=F32)
        a_end_row = acum[ck - 1:ck, :]
        a_end_col = acumt[:, ck - 1:ck]
        e_acum = jnp.exp(acum)
        wt = jnp.exp(a_end_col - acumt) * dtt
        e_end = jnp.exp(a_end_row)
        for g in range(SSD_GROUPS):
            gs = slice(g * SSD_STATE, (g + 1) * SSD_STATE)
            xs_g = xs_ref[rows, g * gw:(g + 1) * gw]
            xs_gb = xs_g.astype(BF16)
            b_g = b_ref[rows, gs]
            c_gb = c_ref[rows, gs].astype(BF16)
            b_gt = b_g.T
            cb = lax.dot_general(c_gb, b_g.astype(BF16), (((1,), (1,)), ((), ())),
                                 preferred_element_type=F32)
            h_g = h_ref[g]
            y_off = jnp.dot(c_gb, h_g.astype(BF16), preferred_element_type=F32)
            y_parts, st_parts, dec_parts = [], [], []
            for j in range(hpg):
                hd = g * hpg + j
                col = acum[:, hd:hd + 1]
                row = acumt[hd:hd + 1, :]
                decay = jnp.exp(jnp.where(causal, col - row, -jnp.inf))
                m = (cb * decay * dtt[hd:hd + 1, :]).astype(BF16)
                y_d = jnp.dot(m, xs_gb, preferred_element_type=F32)
                y_parts.append(y_d + y_off * e_acum[:, hd:hd + 1])
                bw = (b_gt * wt[hd:hd + 1, :]).astype(BF16)
                st_parts.append(jnp.dot(bw, xs_gb, preferred_element_type=F32))
                dec_parts.append(jnp.broadcast_to(e_end[:, hd:hd + 1], (1, gw)))
            y_g = pick_head(y_parts, lane)
            y_ref[rows, g * gw:(g + 1) * gw] = y_g + dskip_ref[:, g * gw:(g + 1) * gw] * xs_g
            h_ref[g] = pick_head(dec_parts, lane1) * h_g + pick_head(st_parts, lane)


def ssd_scan(xbc, dt, dtt, a_log, d_skip, *, d_inner, cps=2):
    bz, s, _ = xbc.shape
    heads = a_log.shape[0]
    gn = SSD_GROUPS * SSD_STATE
    lt = cps * SSD_CHUNK
    xs_blocks = d_inner // gn
    return pl.pallas_call(
        functools.partial(_ssd_scan_kernel, cps=cps),
        out_shape=jax.ShapeDtypeStruct((bz, s, d_inner), F32),
        grid=(bz, s // lt),
        in_specs=[pl.BlockSpec((None, lt, d_inner), lambda b, i: (b, i, 0)),
                  pl.BlockSpec((None, lt, gn), lambda b, i: (b, i, xs_blocks)),
                  pl.BlockSpec((None, lt, gn), lambda b, i: (b, i, xs_blocks + 1)),
                  pl.BlockSpec((None, lt, heads), lambda b, i: (b, i, 0)),
                  pl.BlockSpec((None, heads, lt), lambda b, i: (b, 0, i)),
                  _resident((1, heads)), _resident((heads, 1)), _resident((1, d_inner))],
        out_specs=pl.BlockSpec((None, lt, d_inner), lambda b, i: (b, i, 0)),
        scratch_shapes=[pltpu.VMEM((SSD_GROUPS, SSD_STATE, d_inner // SSD_GROUPS), F32)],
        compiler_params=_params(("parallel", "arbitrary")),
        name="ssd_scan",
    )(xbc, xbc, xbc, dt, dtt, a_log.reshape(1, heads), a_log.reshape(heads, 1),
      jnp.repeat(d_skip, HEAD_DIM).reshape(1, d_inner))


def _ssd_out_kernel(x_ref, y_ref, z_ref, nw_ref, w_ref, g_ref, b_ref, o_ref):
    d_inner = y_ref.shape[1]
    gw = d_inner // SSD_GROUPS
    x = x_ref[...]
    acc = jnp.zeros(x.shape, F32)
    for g in range(SSD_GROUPS):
        sl = slice(g * gw, (g + 1) * gw)
        gated = y_ref[:, sl] * _silu(z_ref[:, sl])
        ms = jnp.mean(gated * gated, axis=-1, keepdims=True)
        normed = (gated * lax.rsqrt(ms + EPS) * nw_ref[:, sl]).astype(BF16)
        acc = acc + jnp.dot(normed, w_ref[sl, :], preferred_element_type=F32)
    o_ref[...] = _layer_norm(ALPHA * x + acc, g_ref[...], b_ref[...])


def ssd_out(x, y, z, norm_w, w_out, g, b, *, tm=512):
    t, d = x.shape
    d_inner = y.shape[1]
    return pl.pallas_call(
        _ssd_out_kernel,
        out_shape=jax.ShapeDtypeStruct((t, d), F32),
        grid=(t // tm,),
        in_specs=[pl.BlockSpec((tm, d), lambda i: (i, 0)),
                  pl.BlockSpec((tm, d_inner), lambda i: (i, 0)),
                  pl.BlockSpec((tm, d_inner), lambda i: (i, 0)),
                  _resident((1, d_inner)), _resident(w_out.shape),
                  _resident((1, d)), _resident((1, d))],
        out_specs=pl.BlockSpec((tm, d), lambda i: (i, 0)),
        compiler_params=_params(("parallel",)),
        name="ssd_out",
    )(x, y, z, norm_w.reshape(1, d_inner), w_out, g.reshape(1, d), b.reshape(1, d))


def _attn_branch_kernel(q_ref, kc_ref, kp_ref, vc_ref, vp_ref, bias_ref, o_ref, lse_ref, *, n):
    heads = bias_ref.shape[0]
    first = pl.program_id(2) == 0
    prev_mask = jnp.where(first, NEG, 0.0).astype(F32)
    lane = lax.broadcasted_iota(jnp.int32, (n, LANES), 1)
    low = lane < HEAD_DIM
    nt = (((1,), (1,)), ((), ()))
    lse_all = jnp.zeros((n, LANES), F32)
    for p in range(heads * HEAD_DIM // LANES):
        sl = slice(p * LANES, (p + 1) * LANES)
        q2 = q_ref[:, sl]
        kc, kp, vc, vp = kc_ref[:, sl], kp_ref[:, sl], vc_ref[:, sl], vp_ref[:, sl]
        outs = []
        for e in range(2):
            hd = 2 * p + e
            qh = jnp.where(low if e == 0 else ~low, q2, jnp.zeros_like(q2))
            lp = lax.dot_general(qh, kp, nt, preferred_element_type=F32)
            lc = lax.dot_general(qh, kc, nt, preferred_element_type=F32)
            lp = lp + bias_ref[hd, :, 0:n] + prev_mask
            lc = lc + bias_ref[hd, :, n:2 * n]
            m = jnp.maximum(jnp.max(lp, axis=-1, keepdims=True),
                            jnp.max(lc, axis=-1, keepdims=True))
            pp = jnp.exp(lp - m)
            pc = jnp.exp(lc - m)
            s = jnp.sum(pp, axis=-1, keepdims=True) + jnp.sum(pc, axis=-1, keepdims=True)
            o = (jnp.dot(pp.astype(BF16), vp, preferred_element_type=F32)
                 + jnp.dot(pc.astype(BF16), vc, preferred_element_type=F32))
            outs.append(o / s)
            lse_all = jnp.where(lane == hd, m + jnp.log(s), lse_all)
        o_ref[:, sl] = jnp.where(low, outs[0], outs[1]).astype(o_ref.dtype)
    lse_ref[...] = lse_all


def attn_branch(q, kv, bias, *, branch, r, n, heads):
    bz, s, qd = q.shape
    w = heads * HEAD_DIM
    nb_q, nb_kv = qd // w, kv.shape[2] // w
    lr = s // r
    qv = q.reshape(bz, lr, r * qd)
    kvv = kv.reshape(bz, lr, r * kv.shape[2])
    blk = lambda col: pl.BlockSpec((None, n, w), col)
    o, lse = pl.pallas_call(
        functools.partial(_attn_branch_kernel, n=n),
        out_shape=(jax.ShapeDtypeStruct((bz, lr, r * w), F32),
                   jax.ShapeDtypeStruct((bz, lr, r * LANES), F32)),
        grid=(bz, r, lr // n),
        in_specs=[blk(lambda b, c, i: (b, i, c * nb_q + branch)),
                  blk(lambda b, c, i: (b, i, c * nb_kv + branch)),
                  blk(lambda b, c, i: (b, jnp.maximum(i - 1, 0), c * nb_kv + branch)),
                  blk(lambda b, c, i: (b, i, c * nb_kv + nb_kv // 2 + branch)),
                  blk(lambda b, c, i: (b, jnp.maximum(i - 1, 0), c * nb_kv + nb_kv // 2 + branch)),
                  _resident(bias.shape)],
        out_specs=(pl.BlockSpec((None, n, w), lambda b, c, i: (b, i, c)),
                   pl.BlockSpec((None, n, LANES), lambda b, c, i: (b, i, c))),
        compiler_params=_params(("parallel", "parallel", "arbitrary")),
        name=f"attn_branch{branch}",
    )(qv, kvv, kvv, kvv, kvv, bias)
    return o.reshape(bz * s, w), lse.reshape(bz * s, LANES)


def _attn_out_kernel(x_ref, o0_ref, o1_ref, o2_ref, l0_ref, l1_ref, l2_ref, e_ref, w_ref,
                     g_ref, b_ref, o_ref):
    l0, l1, l2 = l0_ref[...], l1_ref[...], l2_ref[...]
    m = jnp.maximum(jnp.maximum(l0, l1), l2)
    w0, w1, w2 = jnp.exp(l0 - m), jnp.exp(l1 - m), jnp.exp(l2 - m)
    inv = 1.0 / (w0 + w1 + w2)
    mix = jnp.zeros(o0_ref.shape, F32)
    for wg, og_ref in ((w0, o0_ref), (w1, o1_ref), (w2, o2_ref)):
        wide = jnp.dot(wg * inv, e_ref[...], precision=lax.Precision.HIGHEST,
                       preferred_element_type=F32)
        mix = mix + wide * og_ref[...]
    h = jnp.dot(mix.astype(BF16), w_ref[...], preferred_element_type=F32)
    o_ref[...] = _layer_norm(ALPHA * x_ref[...] + h, g_ref[...], b_ref[...])


def attn_out(x, outs, lses, w_o, g, b, *, heads, tm=256):
    t, d = x.shape
    w = heads * HEAD_DIM
    expand = (jnp.arange(LANES)[:, None] == jnp.arange(w)[None, :] // HEAD_DIM).astype(F32)
    tok = lambda width: pl.BlockSpec((tm, width), lambda i: (i, 0))
    return pl.pallas_call(
        _attn_out_kernel,
        out_shape=jax.ShapeDtypeStruct((t, d), F32),
        grid=(t // tm,),
        in_specs=[tok(d), tok(w), tok(w), tok(w), tok(LANES), tok(LANES), tok(LANES),
                  _resident(expand.shape), _resident(w_o.shape),
                  _resident((1, d)), _resident((1, d))],
        out_specs=tok(d),
        compiler_params=_params(("parallel",)),
        name="attn_out",
    )(x, *outs, *lses, expand, w_o, g.reshape(1, d), b.reshape(1, d))


def _t5_causal_bucket(dist):
    max_exact = NUM_BUCKETS // 2
    logv = (jnp.log(jnp.maximum(dist, 1).astype(F32) / max_exact)
            / math.log(MAX_DISTANCE / max_exact))
    large = jnp.minimum(max_exact + (logv * (NUM_BUCKETS - max_exact)).astype(jnp.int32),
                        NUM_BUCKETS - 1)
    return jnp.where(dist < max_exact, dist, large)


def _branch_bias(rel_bias, branch, r, n, heads):
    buckets = _t5_causal_bucket(jnp.arange(n + 1, dtype=jnp.int32) * r)
    per_dist = rel_bias[buckets][:, branch * heads:(branch + 1) * heads].T
    dist = n + jnp.arange(n)[:, None] - jnp.arange(2 * n)[None, :]
    valid = (dist >= 0) & (dist <= n)
    table = per_dist[:, jnp.clip(dist, 0, n)].astype(F32)
    return jnp.where(valid[None], table, NEG)


def kernel(x, ln_g, ln_b, ffn_w_in, ffn_w_out, m_in_proj, m_conv_w, m_conv_b, m_dt_bias, m_a_log,
           m_d, m_norm_w, m_out_proj, a_w_q, a_w_o, kv_w, rel_bias):
    bz, s, d = x.shape
    t = bz * s
    d_inner = m_out_proj.shape[1]
    conv_dim = m_conv_w.shape[2]
    heads = a_w_o.shape[1] // HEAD_DIM
    x = x.reshape(t, d)
    q_scale = HEAD_DIM ** -0.5
    kv = biases = None
    for layer in range(DEPTH):
        x = ffn_ln(x, ffn_w_in[layer, 0].astype(BF16), ffn_w_out[layer, 0].astype(BF16),
                   ln_g[layer, 0], ln_b[layer, 0])
        if layer < N_SSD_LAYERS:
            w_in = m_in_proj[layer].astype(BF16)
            z, xbc, dt, dtt = ssd_in(x.reshape(bz, s, d), w_in[:, :d_inner],
                                     w_in[:, d_inner:d_inner + conv_dim],
                                     w_in[:, d_inner + conv_dim:], m_conv_w[layer],
                                     m_conv_b[layer], m_dt_bias[layer])
            y = ssd_scan(xbc, dt, dtt, m_a_log[layer], m_d[layer], d_inner=d_inner)
            x = ssd_out(x, y.reshape(t, d_inner), z.reshape(t, d_inner), m_norm_w[layer],
                        m_out_proj[layer].astype(BF16), ln_g[layer, 1], ln_b[layer, 1])
        else:
            a = layer - N_SSD_LAYERS
            q = proj(x, a_w_q[a].astype(BF16), scale=q_scale).reshape(bz, s, -1)
            outs, lses = [], []
            for branch, (window, r) in enumerate(ATTN_BRANCHES):
                o, lse = attn_branch(q, kv, biases[branch], branch=branch, r=r, n=window // r,
                                     heads=heads)
                outs.append(o)
                lses.append(lse)
            x = attn_out(x, outs, lses, a_w_o[a].astype(BF16), ln_g[layer, 1], ln_b[layer, 1],
                         heads=heads)
        x = ffn_ln(x, ffn_w_in[layer, 1].astype(BF16), ffn_w_out[layer, 1].astype(BF16),
                   ln_g[layer, 2], ln_b[layer, 2])
        if layer == N_SSD_LAYERS - 1:
            kv = proj(x, kv_w.astype(BF16)).reshape(bz, s, -1)
            biases = [_branch_bias(rel_bias, branch, r, window // r, heads)
                      for branch, (window, r) in enumerate(ATTN_BRANCHES)]
    return x.reshape(bz, s, d)
```

```python
import functools
import math

import jax
import jax.numpy as jnp
from jax import lax
from jax.experimental import pallas as pl
from jax.experimental.pallas import tpu as pltpu

F32 = jnp.float32
BF16 = jnp.bfloat16

DEPTH = 4
N_SSD_LAYERS = DEPTH // 2
ALPHA = (2.0 * DEPTH) ** 0.25
EPS = 1e-5
FFN_RES = 0.5
HEAD_DIM = 64
SSD_GROUPS = 8
SSD_STATE = 128
SSD_CHUNK = 128
SSD_CONV = 4
ATTN_BRANCHES = ((128, 1), (512, 4), (2048, 16))
ATTN_WINDOW = 128
ATTN_TILE = ATTN_WINDOW * max(r for _, r in ATTN_BRANCHES)
NUM_BUCKETS = 32
MAX_DISTANCE = 2048
NEG = -1e30

LANES = 128
SUBLANES = 8
VMEM_LIMIT_BYTES = 56 * 1024 * 1024

NT_DIMS = (((1,), (1,)), ((), ()))


def _params(semantics):
    return pltpu.CompilerParams(dimension_semantics=semantics,
                                vmem_limit_bytes=VMEM_LIMIT_BYTES)


def _resident(shape):
    zeros = (0,) * len(shape)
    return pl.BlockSpec(shape, lambda *_: zeros, pipeline_mode=pl.Buffered(1))


def _layer_norm(y, g, b):
    mu = jnp.mean(y, axis=-1, keepdims=True)
    d = y - mu
    var = jnp.mean(d * d, axis=-1, keepdims=True)
    return d * lax.rsqrt(var + EPS) * g + b


def _silu(v):
    return v * jax.nn.sigmoid(v)


def _ffn_kernel(x_ref, win_ref, wout_ref, g_ref, b_ref, o_ref, *, d_ff, ck):
    x = x_ref[...]
    xb = x.astype(BF16)
    acc = jnp.zeros(x.shape, F32)
    for c in range(d_ff // ck):
        gate = jnp.dot(xb, win_ref[:, c * ck:(c + 1) * ck], preferred_element_type=F32)
        up = jnp.dot(xb, win_ref[:, d_ff + c * ck:d_ff + (c + 1) * ck],
                     preferred_element_type=F32)
        act = (_silu(gate) * up).astype(BF16)
        acc = acc + jnp.dot(act, wout_ref[c * ck:(c + 1) * ck, :], preferred_element_type=F32)
    o_ref[...] = _layer_norm(ALPHA * x + FFN_RES * acc, g_ref[...], b_ref[...])


def ffn_ln(x, w_in, w_out, g, b, *, tm=512, ck=256):
    t, d = x.shape
    d_ff = w_out.shape[0]
    return pl.pallas_call(
        functools.partial(_ffn_kernel, d_ff=d_ff, ck=ck),
        out_shape=jax.ShapeDtypeStruct((t, d), F32),
        grid=(t // tm,),
        in_specs=[pl.BlockSpec((tm, d), lambda i: (i, 0)),
                  _resident(w_in.shape), _resident(w_out.shape),
                  _resident((1, d)), _resident((1, d))],
        out_specs=pl.BlockSpec((tm, d), lambda i: (i, 0)),
        compiler_params=_params(("parallel",)),
        name="ffn_ln",
    )(x, w_in, w_out, g.reshape(1, d), b.reshape(1, d))


def _proj_cm_kernel(x_ref, w_ref, *refs, dilations, width, scale):
    o_refs, scr = refs[:-1], refs[-1]
    xb = x_ref[...].astype(BF16)
    tm = xb.shape[0]
    for n, (o_ref, r) in enumerate(zip(o_refs, dilations)):
        y = jnp.dot(xb, w_ref[:, n * width:(n + 1) * width], preferred_element_type=F32) * scale
        if r == 1:
            for lg in range(width // LANES):
                o_ref[lg, 0] = y[:, lg * LANES:(lg + 1) * LANES].astype(BF16)
        else:
            for lg in range(width // LANES):
                scr[lg] = y[:, lg * LANES:(lg + 1) * LANES]
            for lg in range(width // LANES):
                for c in range(r):
                    o_ref[lg, c] = scr[lg, pl.ds(c, tm // r, stride=r), :].astype(BF16)


def proj_cm(x, w, *, dilations, width, scale=1.0, tm=512):
    bz, s, d = x.shape
    groups = width // LANES
    kern = functools.partial(_proj_cm_kernel, dilations=dilations, width=width, scale=scale)
    return pl.pallas_call(
        kern,
        out_shape=tuple(jax.ShapeDtypeStruct((bz, groups, r, s // r, LANES), BF16)
                        for r in dilations),
        grid=(bz, s // tm),
        in_specs=[pl.BlockSpec((None, tm, d), lambda b, i: (b, i, 0)), _resident(w.shape)],
        out_specs=tuple(pl.BlockSpec((None, groups, r, tm // r, LANES),
                                     lambda b, i: (b, 0, 0, i, 0)) for r in dilations),
        scratch_shapes=[pltpu.VMEM((groups, tm, LANES), F32)],
        compiler_params=_params(("parallel", "parallel")),
        name="proj_cm",
    )(x, w)


def _ssd_in_kernel(x_ref, wz_ref, wxbc_ref, wdt_ref, wdtt_ref, cw_ref, cb_ref, dtb_ref, dtbt_ref,
                   z_ref, xbc_ref, dt_ref, dtt_ref, halo_ref, *, tm, cn):
    i = pl.program_id(1)
    xb = x_ref[...].astype(BF16)
    hist = SUBLANES

    @pl.when(i == 0)
    def _():
        halo_ref[0:hist, :] = jnp.zeros((hist, halo_ref.shape[1]), F32)

    @pl.when(i > 0)
    def _():
        halo_ref[0:hist, :] = halo_ref[tm:tm + hist, :]

    for c in range(wz_ref.shape[1] // cn):
        sl = slice(c * cn, (c + 1) * cn)
        z_ref[:, sl] = jnp.dot(xb, wz_ref[:, sl], preferred_element_type=F32).astype(z_ref.dtype)
    for c in range(wxbc_ref.shape[1] // cn):
        sl = slice(c * cn, (c + 1) * cn)
        halo_ref[hist:hist + tm, sl] = jnp.dot(xb, wxbc_ref[:, sl], preferred_element_type=F32)
        conv = cb_ref[:, sl] + jnp.zeros((tm, cn), F32)
        for k in range(SSD_CONV):
            off = hist - (SSD_CONV - 1) + k
            conv = conv + cw_ref[k:k + 1, sl] * halo_ref[off:off + tm, sl]
        xbc_ref[:, sl] = _silu(conv).astype(xbc_ref.dtype)
    dt_raw = jnp.dot(xb, wdt_ref[...], preferred_element_type=F32)
    dt_ref[...] = jax.nn.softplus(dt_raw + dtb_ref[...])
    dtt_raw = lax.dot_general(wdtt_ref[...], xb, NT_DIMS, preferred_element_type=F32)
    dtt_ref[...] = jax.nn.softplus(dtt_raw + dtbt_ref[...])


def ssd_in(x, w_z, w_xbc, w_dt, conv_w, conv_b, dt_bias, *, tm=256, cn=512):
    bz, s, d = x.shape
    d_inner, conv_dim, heads = w_z.shape[1], w_xbc.shape[1], w_dt.shape[1]
    kern = functools.partial(_ssd_in_kernel, tm=tm, cn=cn)
    return pl.pallas_call(
        kern,
        out_shape=(jax.ShapeDtypeStruct((bz, s, d_inner), BF16),
                   jax.ShapeDtypeStruct((bz, s, conv_dim), BF16),
                   jax.ShapeDtypeStruct((bz, s, heads), F32),
                   jax.ShapeDtypeStruct((bz, heads, s), F32)),
        grid=(bz, s // tm),
        in_specs=[pl.BlockSpec((None, tm, d), lambda b, i: (b, i, 0)),
                  _resident(w_z.shape), _resident(w_xbc.shape), _resident(w_dt.shape),
                  _resident((heads, d)), _resident(conv_w.shape), _resident((1, conv_dim)),
                  _resident((1, heads)), _resident((heads, 1))],
        out_specs=(pl.BlockSpec((None, tm, d_inner), lambda b, i: (b, i, 0)),
                   pl.BlockSpec((None, tm, conv_dim), lambda b, i: (b, i, 0)),
                   pl.BlockSpec((None, tm, heads), lambda b, i: (b, i, 0)),
                   pl.BlockSpec((None, heads, tm), lambda b, i: (b, 0, i))),
        scratch_shapes=[pltpu.VMEM((tm + SUBLANES, conv_dim), F32)],
        compiler_params=_params(("parallel", "arbitrary")),
        name="ssd_in",
    )(x, w_z, w_xbc, w_dt, w_dt.T, conv_w, conv_b.reshape(1, conv_dim),
      dt_bias.reshape(1, heads), dt_bias.reshape(heads, 1))


def _ssd_scan_kernel(xs_ref, b_ref, c_ref, dt_ref, dtt_ref, alog_ref, alogt_ref, dskip_ref,
                     y_ref, h_ref, *, cps):
    ck = SSD_CHUNK
    hpg = h_ref.shape[2] // HEAD_DIM
    gw = hpg * HEAD_DIM

    @pl.when(pl.program_id(1) == 0)
    def _():
        h_ref[...] = jnp.zeros(h_ref.shape, F32)

    a_row = -jnp.exp(alog_ref[...])
    a_col = -jnp.exp(alogt_ref[...])
    ri = lax.broadcasted_iota(jnp.int32, (ck, ck), 0)
    ci = lax.broadcasted_iota(jnp.int32, (ck, ck), 1)
    causal = ci <= ri
    tril = causal.astype(F32)
    triu = (ri <= ci).astype(F32)
    lane = lax.broadcasted_iota(jnp.int32, (ck, gw), 1)
    lane1 = lax.broadcasted_iota(jnp.int32, (1, gw), 1)

    def pick_head(vals, lanes):
        out = vals[hpg - 1]
        for j in range(hpg - 2, -1, -1):
            out = jnp.where(lanes < (j + 1) * HEAD_DIM, vals[j], out)
        return out

    for q in range(cps):
        rows = slice(q * ck, (q + 1) * ck)
        dt = dt_ref[rows, :]
        dtt = dtt_ref[:, rows]
        acum = jnp.dot(tril, dt * a_row, precision=lax.Precision.HIGHEST,
                       preferred_element_type=F32)
        acumt = jnp.dot(dtt * a_col, triu, precision=lax.Precision.HIGHEST,
                        preferred_element_type=F32)
        a_end_row = acum[ck - 1:ck, :]
        a_end_col = acumt[:, ck - 1:ck]
        e_acum = jnp.exp(acum)
        wt = jnp.exp(a_end_col - acumt) * dtt
        e_end = jnp.exp(a_end_row)
        for g in range(SSD_GROUPS):
            gs = slice(g * SSD_STATE, (g + 1) * SSD_STATE)
            xs_gb = xs_ref[rows, g * gw:(g + 1) * gw]
            b_gb = b_ref[rows, gs]
            c_gb = c_ref[rows, gs]
            b_gt = b_gb.astype(F32).T
            cb = lax.dot_general(c_gb, b_gb, NT_DIMS, preferred_element_type=F32)
            h_g = h_ref[g]
            y_off = jnp.dot(c_gb, h_g.astype(BF16), preferred_element_type=F32)
            y_parts, st_parts, dec_parts = [], [], []
            for j in range(hpg):
                hd = g * hpg + j
                col = acum[:, hd:hd + 1]
                row = acumt[hd:hd + 1, :]
                decay = jnp.exp(jnp.where(causal, col - row, -jnp.inf))
                m = (cb * decay * dtt[hd:hd + 1, :]).astype(BF16)
                y_d = jnp.dot(m, xs_gb, preferred_element_type=F32)
                y_parts.append(y_d + y_off * e_acum[:, hd:hd + 1])
                bw = (b_gt * wt[hd:hd + 1, :]).astype(BF16)
                st_parts.append(jnp.dot(bw, xs_gb, preferred_element_type=F32))
                dec_parts.append(jnp.broadcast_to(e_end[:, hd:hd + 1], (1, gw)))
            y_g = pick_head(y_parts, lane)
            y_g = y_g + dskip_ref[:, g * gw:(g + 1) * gw] * xs_gb.astype(F32)
            y_ref[rows, g * gw:(g + 1) * gw] = y_g.astype(y_ref.dtype)
            h_ref[g] = pick_head(dec_parts, lane1) * h_g + pick_head(st_parts, lane)


def ssd_scan(xbc, dt, dtt, a_log, d_skip, *, d_inner, cps=2):
    bz, s, _ = xbc.shape
    heads = a_log.shape[0]
    gn = SSD_GROUPS * SSD_STATE
    lt = cps * SSD_CHUNK
    xs_blocks = d_inner // gn
    return pl.pallas_call(
        functools.partial(_ssd_scan_kernel, cps=cps),
        out_shape=jax.ShapeDtypeStruct((bz, s, d_inner), BF16),
        grid=(bz, s // lt),
        in_specs=[pl.BlockSpec((None, lt, d_inner), lambda b, i: (b, i, 0)),
                  pl.BlockSpec((None, lt, gn), lambda b, i: (b, i, xs_blocks)),
                  pl.BlockSpec((None, lt, gn), lambda b, i: (b, i, xs_blocks + 1)),
                  pl.BlockSpec((None, lt, heads), lambda b, i: (b, i, 0)),
                  pl.BlockSpec((None, heads, lt), lambda b, i: (b, 0, i)),
                  _resident((1, heads)), _resident((heads, 1)), _resident((1, d_inner))],
        out_specs=pl.BlockSpec((None, lt, d_inner), lambda b, i: (b, i, 0)),
        scratch_shapes=[pltpu.VMEM((SSD_GROUPS, SSD_STATE, d_inner // SSD_GROUPS), F32)],
        compiler_params=_params(("parallel", "arbitrary")),
        name="ssd_scan",
    )(xbc, xbc, xbc, dt, dtt, a_log.reshape(1, heads), a_log.reshape(heads, 1),
      jnp.repeat(d_skip, HEAD_DIM).reshape(1, d_inner))


def _ssd_out_kernel(x_ref, y_ref, z_ref, nw_ref, w_ref, g_ref, b_ref, o_ref):
    d_inner = y_ref.shape[1]
    gw = d_inner // SSD_GROUPS
    x = x_ref[...]
    acc = jnp.zeros(x.shape, F32)
    for g in range(SSD_GROUPS):
        sl = slice(g * gw, (g + 1) * gw)
        gated = y_ref[:, sl].astype(F32) * _silu(z_ref[:, sl].astype(F32))
        ms = jnp.mean(gated * gated, axis=-1, keepdims=True)
        normed = (gated * lax.rsqrt(ms + EPS) * nw_ref[:, sl]).astype(BF16)
        acc = acc + jnp.dot(normed, w_ref[sl, :], preferred_element_type=F32)
    o_ref[...] = _layer_norm(ALPHA * x + acc, g_ref[...], b_ref[...])


def ssd_out(x, y, z, norm_w, w_out, g, b, *, tm=512):
    t, d = x.shape
    d_inner = y.shape[1]
    return pl.pallas_call(
        _ssd_out_kernel,
        out_shape=jax.ShapeDtypeStruct((t, d), F32),
        grid=(t // tm,),
        in_specs=[pl.BlockSpec((tm, d), lambda i: (i, 0)),
                  pl.BlockSpec((tm, d_inner), lambda i: (i, 0)),
                  pl.BlockSpec((tm, d_inner), lambda i: (i, 0)),
                  _resident((1, d_inner)), _resident(w_out.shape),
                  _resident((1, d)), _resident((1, d))],
        out_specs=pl.BlockSpec((tm, d), lambda i: (i, 0)),
        compiler_params=_params(("parallel",)),
        name="ssd_out",
    )(x, y, z, norm_w.reshape(1, d_inner), w_out, g.reshape(1, d), b.reshape(1, d))


def _attn_kernel(*refs, dilations):
    nb = len(dilations)
    n = ATTN_WINDOW
    q_refs = refs[0:nb]
    kc_refs, kp_refs = refs[nb:3 * nb:2], refs[nb + 1:3 * nb:2]
    vc_refs, vp_refs = refs[3 * nb:5 * nb:2], refs[3 * nb + 1:5 * nb:2]
    brev_ref, bfar_ref, o_ref = refs[5 * nb:5 * nb + 3]
    scratch = refs[5 * nb + 3:]
    kw_refs, vw_refs = scratch[0:nb], scratch[nb:2 * nb]
    bias_scr, o_scr, l_scr = scratch[2 * nb:]
    first_tile = pl.program_id(2) == 0

    ri = lax.broadcasted_iota(jnp.int32, (n, n), 0)
    ki = lax.broadcasted_iota(jnp.int32, (n, n), 1)
    low_q = lax.broadcasted_iota(jnp.int32, (n, LANES), 1) < HEAD_DIM
    low_v = lax.broadcasted_iota(jnp.int32, (2 * n, LANES), 1) < HEAD_DIM
    in_prev = lax.broadcasted_iota(jnp.int32, (1, 2 * n), 1) < n

    for g in range(nb):
        for e in range(2):
            circ = pltpu.roll(jnp.broadcast_to(brev_ref[g, e:e + 1, :], (n, n)), 0, 1,
                              stride=1, stride_axis=0)
            far = bfar_ref[g, e:e + 1, :]
            bias_scr[g, e, :, 0:n] = jnp.where(ki > ri, circ, jnp.where(ki == ri, far, NEG))
            bias_scr[g, e, :, n:2 * n] = jnp.where(ki <= ri, circ, NEG)

    for g in range(nb):
        kw_refs[g][:, 0:n, :] = kp_refs[g][...]
        kw_refs[g][:, n:, :] = kc_refs[g][...]
        vw_refs[g][:, 0:n, :] = vp_refs[g][...]
        vw_refs[g][:, n:, :] = vc_refs[g][...]

    one = jnp.ones((), BF16)
    zero = jnp.zeros((), BF16)
    for g, r in enumerate(dilations):
        blocks = ATTN_TILE // r // n

        def unit(idx, carry, g=g, r=r, blocks=blocks):
            c, j = idx // blocks, idx % blocks
            start = pl.multiple_of(j * n, n)
            q_u = q_refs[g][c, pl.ds(start, n), :]
            k_w = kw_refs[g][c, pl.ds(start, 2 * n), :]
            v_w = vw_refs[g][c, pl.ds(start, 2 * n), :]
            no_prev = jnp.logical_and(first_tile, j == 0)
            neg_row = jnp.where(in_prev, jnp.where(no_prev, NEG, 0.0), 0.0)
            res, mx = [], []
            for e in range(2):
                own_q = low_q if e == 0 else ~low_q
                own_v = low_v if e == 0 else ~low_v
                logits = lax.dot_general(jnp.where(own_q, q_u, zero), k_w, NT_DIMS,
                                         preferred_element_type=F32)
                logits = logits + bias_scr[g, e] + neg_row
                m = jnp.max(logits, axis=-1, keepdims=True)
                p = jnp.exp(logits - m).astype(BF16)
                res.append(jnp.dot(p, jnp.where(own_v, v_w, one), preferred_element_type=F32))
                mx.append(m)
            num = jnp.where(low_q, res[0], res[1])
            den = pltpu.roll(jnp.where(low_q, res[1], res[0]), HEAD_DIM, 1)
            rows = pl.ds(j * (n * r) + c, n, stride=r) if r > 1 else pl.ds(start, n)
            o_scr[g, rows, :] = num / den
            l_scr[g, rows, :] = jnp.where(low_q, mx[0], mx[1]) + jnp.log(den)
            return carry

        lax.fori_loop(0, r * blocks, unit, 0, unroll=2)

    rb = 2 * n

    def mix(tb, carry):
        rows = pl.ds(pl.multiple_of(tb * rb, rb), rb)
        ls = [l_scr[g, rows, :] for g in range(nb)]
        top = functools.reduce(jnp.maximum, ls)
        ws = [jnp.exp(l - top) for l in ls]
        acc = functools.reduce(lambda a, b: a + b, [w * o_scr[g, rows, :] for g, w in enumerate(ws)])
        o_ref[rows, :] = (acc / functools.reduce(lambda a, b: a + b, ws)).astype(o_ref.dtype)
        return carry

    lax.fori_loop(0, ATTN_TILE // rb, mix, 0)


def attn_mix(qs, ks, vs, brev, bfar, *, dilations):
    bz, groups, _, s, _ = qs[0].shape
    n, tile = ATTN_WINDOW, ATTN_TILE
    nb = len(dilations)

    def cur(r):
        return pl.BlockSpec((None, None, r, tile // r, LANES), lambda b, hg, i: (b, hg, 0, i, 0))

    def prev(r):
        per_tile = tile // r // n
        return pl.BlockSpec((None, None, r, n, LANES),
                            lambda b, hg, i: (b, hg, 0, jnp.maximum(i * per_tile - 1, 0), 0))

    in_specs = [cur(r) for r in dilations]
    args = list(qs)
    for arrs in (ks, vs):
        for a, r in zip(arrs, dilations):
            in_specs += [cur(r), prev(r)]
            args += [a, a]
    in_specs += [pl.BlockSpec((nb, None, 2, LANES), lambda b, hg, i: (0, hg, 0, 0)),
                 pl.BlockSpec((nb, None, 2, 1), lambda b, hg, i: (0, hg, 0, 0))]
    args += [brev, bfar]
    windows = [pltpu.VMEM((r, n + tile // r, LANES), BF16) for r in dilations]
    return pl.pallas_call(
        functools.partial(_attn_kernel, dilations=dilations),
        out_shape=jax.ShapeDtypeStruct((bz, groups, s, LANES), BF16),
        grid=(bz, groups, s // tile),
        in_specs=in_specs,
        out_specs=pl.BlockSpec((None, None, tile, LANES), lambda b, hg, i: (b, hg, i, 0)),
        scratch_shapes=windows + windows + [pltpu.VMEM((nb, 2, n, 2 * n), F32),
                                            pltpu.VMEM((nb, tile, LANES), F32),
                                            pltpu.VMEM((nb, tile, LANES), F32)],
        compiler_params=_params(("parallel", "parallel", "arbitrary")),
        name="attn_mix",
    )(*args)


def _attn_out_kernel(x_ref, o_ref, w_ref, g_ref, b_ref, out_ref):
    mix = jnp.concatenate([o_ref[lg] for lg in range(o_ref.shape[0])], axis=-1)
    h = jnp.dot(mix, w_ref[...], preferred_element_type=F32)
    out_ref[...] = _layer_norm(ALPHA * x_ref[...] + h, g_ref[...], b_ref[...])


def attn_out(x, o, w_o, g, b, *, tm=512):
    bz, s, d = x.shape
    groups = o.shape[1]
    return pl.pallas_call(
        _attn_out_kernel,
        out_shape=jax.ShapeDtypeStruct((bz, s, d), F32),
        grid=(bz, s // tm),
        in_specs=[pl.BlockSpec((None, tm, d), lambda b, i: (b, i, 0)),
                  pl.BlockSpec((None, groups, tm, LANES), lambda b, i: (b, 0, i, 0)),
                  _resident(w_o.shape), _resident((1, d)), _resident((1, d))],
        out_specs=pl.BlockSpec((None, tm, d), lambda b, i: (b, i, 0)),
        compiler_params=_params(("parallel", "parallel")),
        name="attn_out",
    )(x, o, w_o, g.reshape(1, d), b.reshape(1, d))


def _t5_causal_bucket(dist):
    max_exact = NUM_BUCKETS // 2
    logv = (jnp.log(jnp.maximum(dist, 1).astype(F32) / max_exact)
            / math.log(MAX_DISTANCE / max_exact))
    large = jnp.minimum(max_exact + (logv * (NUM_BUCKETS - max_exact)).astype(jnp.int32),
                        NUM_BUCKETS - 1)
    return jnp.where(dist < max_exact, dist, large)


def _bias_rows(rel_bias, heads):
    n = ATTN_WINDOW
    rev, far = [], []
    for branch, (window, r) in enumerate(ATTN_BRANCHES):
        buckets = _t5_causal_bucket(jnp.arange(n + 1, dtype=jnp.int32) * r)
        per_dist = rel_bias[buckets][:, branch * heads:(branch + 1) * heads].T.astype(F32)
        rev.append(per_dist[:, (-jnp.arange(n)) % n])
        far.append(per_dist[:, n:n + 1])
    return (jnp.stack(rev).reshape(len(rev), heads // 2, 2, n),
            jnp.stack(far).reshape(len(far), heads // 2, 2, 1))


def kernel(x, ln_g, ln_b, ffn_w_in, ffn_w_out, m_in_proj, m_conv_w, m_conv_b, m_dt_bias, m_a_log,
           m_d, m_norm_w, m_out_proj, a_w_q, a_w_o, kv_w, rel_bias):
    bz, s, d = x.shape
    t = bz * s
    d_inner = m_out_proj.shape[1]
    conv_dim = m_conv_w.shape[2]
    width = a_w_o.shape[1]
    heads = width // HEAD_DIM
    dilations = tuple(r for _, r in ATTN_BRANCHES)
    nb = len(dilations)
    assert all(window // r == ATTN_WINDOW for window, r in ATTN_BRANCHES)
    x = x.reshape(t, d)
    ks = vs = brev = bfar = None
    for layer in range(DEPTH):
        x = ffn_ln(x, ffn_w_in[layer, 0].astype(BF16), ffn_w_out[layer, 0].astype(BF16),
                   ln_g[layer, 0], ln_b[layer, 0])
        if layer < N_SSD_LAYERS:
            w_in = m_in_proj[layer].astype(BF16)
            z, xbc, dt, dtt = ssd_in(x.reshape(bz, s, d), w_in[:, :d_inner],
                                     w_in[:, d_inner:d_inner + conv_dim],
                                     w_in[:, d_inner + conv_dim:], m_conv_w[layer],
                                     m_conv_b[layer], m_dt_bias[layer])
            y = ssd_scan(xbc, dt, dtt, m_a_log[layer], m_d[layer], d_inner=d_inner)
            x = ssd_out(x, y.reshape(t, d_inner), z.reshape(t, d_inner), m_norm_w[layer],
                        m_out_proj[layer].astype(BF16), ln_g[layer, 1], ln_b[layer, 1])
        else:
            a = layer - N_SSD_LAYERS
            qs = proj_cm(x.reshape(bz, s, d), a_w_q[a].astype(BF16), dilations=dilations,
                         width=width, scale=HEAD_DIM ** -0.5)
            o = attn_mix(qs, ks, vs, brev, bfar, dilations=dilations)
            x = attn_out(x.reshape(bz, s, d), o, a_w_o[a].astype(BF16), ln_g[layer, 1],
                         ln_b[layer, 1]).reshape(t, d)
        x = ffn_ln(x, ffn_w_in[layer, 1].astype(BF16), ffn_w_out[layer, 1].astype(BF16),
                   ln_g[layer, 2], ln_b[layer, 2])
        if layer == N_SSD_LAYERS - 1:
            kvs = proj_cm(x.reshape(bz, s, d), kv_w.astype(BF16), dilations=dilations * 2,
                          width=width)
            ks, vs = kvs[:nb], kvs[nb:]
            brev, bfar = _bias_rows(rel_bias, heads)
    return x.reshape(bz, s, d)
```

```python
import functools
import math

import jax
import jax.numpy as jnp
from jax import lax
from jax.experimental import pallas as pl
from jax.experimental.pallas import tpu as pltpu

F32 = jnp.float32
BF16 = jnp.bfloat16

DEPTH = 4
N_SSD_LAYERS = DEPTH // 2
ALPHA = (2.0 * DEPTH) ** 0.25
EPS = 1e-5
FFN_RES = 0.5
HEAD_DIM = 64
SSD_GROUPS = 8
SSD_STATE = 128
SSD_CHUNK = 128
SSD_CONV = 4
ATTN_BRANCHES = ((128, 1), (512, 4), (2048, 16))
ATTN_WINDOW = 128
ATTN_TILE = ATTN_WINDOW * max(r for _, r in ATTN_BRANCHES)
NUM_BUCKETS = 32
MAX_DISTANCE = 2048
NEG = -1e30

LANES = 128
SUBLANES = 8
VMEM_LIMIT_BYTES = 56 * 1024 * 1024

NT_DIMS = (((1,), (1,)), ((), ()))


def _params(semantics):
    return pltpu.CompilerParams(dimension_semantics=semantics,
                                vmem_limit_bytes=VMEM_LIMIT_BYTES)


def _resident(shape):
    zeros = (0,) * len(shape)
    return pl.BlockSpec(shape, lambda *_: zeros, pipeline_mode=pl.Buffered(1))


def _layer_norm(y, g, b):
    mu = jnp.mean(y, axis=-1, keepdims=True)
    d = y - mu
    var = jnp.mean(d * d, axis=-1, keepdims=True)
    return d * lax.rsqrt(var + EPS) * g + b


def _silu(v):
    return v * jax.nn.sigmoid(v)


def _ffn_kernel(x_ref, win_ref, wout_ref, g_ref, b_ref, o_ref, *, d_ff, ck):
    x = x_ref[...]
    xb = x.astype(BF16)
    acc = jnp.zeros(x.shape, F32)
    for c in range(d_ff // ck):
        gate = jnp.dot(xb, win_ref[:, c * ck:(c + 1) * ck], preferred_element_type=F32)
        up = jnp.dot(xb, win_ref[:, d_ff + c * ck:d_ff + (c + 1) * ck],
                     preferred_element_type=F32)
        act = (_silu(gate) * up).astype(BF16)
        acc = acc + jnp.dot(act, wout_ref[c * ck:(c + 1) * ck, :], preferred_element_type=F32)
    o_ref[...] = _layer_norm(ALPHA * x + FFN_RES * acc, g_ref[...], b_ref[...])


def ffn_ln(x, w_in, w_out, g, b, *, tm=512, ck=256):
    t, d = x.shape
    d_ff = w_out.shape[0]
    return pl.pallas_call(
        functools.partial(_ffn_kernel, d_ff=d_ff, ck=ck),
        out_shape=jax.ShapeDtypeStruct((t, d), F32),
        grid=(t // tm,),
        in_specs=[pl.BlockSpec((tm, d), lambda i: (i, 0)),
                  _resident(w_in.shape), _resident(w_out.shape),
                  _resident((1, d)), _resident((1, d))],
        out_specs=pl.BlockSpec((tm, d), lambda i: (i, 0)),
        compiler_params=_params(("parallel",)),
        name="ffn_ln",
    )(x, w_in, w_out, g.reshape(1, d), b.reshape(1, d))


def _proj_cm_kernel(x_ref, w_ref, *refs, dilations, width, scale):
    o_refs, scr = refs[:-1], refs[-1]
    xb = x_ref[...].astype(BF16)
    tm = xb.shape[0]
    for n, (o_ref, r) in enumerate(zip(o_refs, dilations)):
        y = jnp.dot(xb, w_ref[:, n * width:(n + 1) * width], preferred_element_type=F32) * scale
        if r == 1:
            for lg in range(width // LANES):
                o_ref[lg, 0] = y[:, lg * LANES:(lg + 1) * LANES].astype(BF16)
        else:
            for lg in range(width // LANES):
                scr[lg] = y[:, lg * LANES:(lg + 1) * LANES]
            for lg in range(width // LANES):
                for c in range(r):
                    o_ref[lg, c] = scr[lg, pl.ds(c, tm // r, stride=r), :].astype(BF16)


def proj_cm(x, w, *, dilations, width, scale=1.0, tm=512):
    bz, s, d = x.shape
    groups = width // LANES
    kern = functools.partial(_proj_cm_kernel, dilations=dilations, width=width, scale=scale)
    return pl.pallas_call(
        kern,
        out_shape=tuple(jax.ShapeDtypeStruct((bz, groups, r, s // r, LANES), BF16)
                        for r in dilations),
        grid=(bz, s // tm),
        in_specs=[pl.BlockSpec((None, tm, d), lambda b, i: (b, i, 0)), _resident(w.shape)],
        out_specs=tuple(pl.BlockSpec((None, groups, r, tm // r, LANES),
                                     lambda b, i: (b, 0, 0, i, 0)) for r in dilations),
        scratch_shapes=[pltpu.VMEM((groups, tm, LANES), F32)],
        compiler_params=_params(("parallel", "parallel")),
        name="proj_cm",
    )(x, w)


def _ssd_in_kernel(x_ref, wz_ref, wxbc_ref, wdt_ref, wdtt_ref, cw_ref, cb_ref, dtb_ref, dtbt_ref,
                   z_ref, xbc_ref, dt_ref, dtt_ref, halo_ref, *, tm, cn):
    i = pl.program_id(1)
    xb = x_ref[...].astype(BF16)
    hist = SUBLANES

    @pl.when(i == 0)
    def _():
        halo_ref[0:hist, :] = jnp.zeros((hist, halo_ref.shape[1]), F32)

    @pl.when(i > 0)
    def _():
        halo_ref[0:hist, :] = halo_ref[tm:tm + hist, :]

    for c in range(wz_ref.shape[1] // cn):
        sl = slice(c * cn, (c + 1) * cn)
        z_ref[:, sl] = jnp.dot(xb, wz_ref[:, sl], preferred_element_type=F32).astype(z_ref.dtype)
    for c in range(wxbc_ref.shape[1] // cn):
        sl = slice(c * cn, (c + 1) * cn)
        halo_ref[hist:hist + tm, sl] = jnp.dot(xb, wxbc_ref[:, sl], preferred_element_type=F32)
        conv = cb_ref[:, sl] + jnp.zeros((tm, cn), F32)
        for k in range(SSD_CONV):
            off = hist - (SSD_CONV - 1) + k
            conv = conv + cw_ref[k:k + 1, sl] * halo_ref[off:off + tm, sl]
        xbc_ref[:, sl] = _silu(conv).astype(xbc_ref.dtype)
    dt_raw = jnp.dot(xb, wdt_ref[...], preferred_element_type=F32)
    dt_ref[...] = jax.nn.softplus(dt_raw + dtb_ref[...])
    dtt_raw = lax.dot_general(wdtt_ref[...], xb, NT_DIMS, preferred_element_type=F32)
    dtt_ref[...] = jax.nn.softplus(dtt_raw + dtbt_ref[...])


def ssd_in(x, w_z, w_xbc, w_dt, conv_w, conv_b, dt_bias, *, tm=256, cn=512):
    bz, s, d = x.shape
    d_inner, conv_dim, heads = w_z.shape[1], w_xbc.shape[1], w_dt.shape[1]
    kern = functools.partial(_ssd_in_kernel, tm=tm, cn=cn)
    return pl.pallas_call(
        kern,
        out_shape=(jax.ShapeDtypeStruct((bz, s, d_inner), BF16),
                   jax.ShapeDtypeStruct((bz, s, conv_dim), BF16),
                   jax.ShapeDtypeStruct((bz, s, heads), F32),
                   jax.ShapeDtypeStruct((bz, heads, s), F32)),
        grid=(bz, s // tm),
        in_specs=[pl.BlockSpec((None, tm, d), lambda b, i: (b, i, 0)),
                  _resident(w_z.shape), _resident(w_xbc.shape), _resident(w_dt.shape),
                  _resident((heads, d)), _resident(conv_w.shape), _resident((1, conv_dim)),
                  _resident((1, heads)), _resident((heads, 1))],
        out_specs=(pl.BlockSpec((None, tm, d_inner), lambda b, i: (b, i, 0)),
                   pl.BlockSpec((None, tm, conv_dim), lambda b, i: (b, i, 0)),
                   pl.BlockSpec((None, tm, heads), lambda b, i: (b, i, 0)),
                   pl.BlockSpec((None, heads, tm), lambda b, i: (b, 0, i))),
        scratch_shapes=[pltpu.VMEM((tm + SUBLANES, conv_dim), F32)],
        compiler_params=_params(("parallel", "arbitrary")),
        name="ssd_in",
    )(x, w_z, w_xbc, w_dt, w_dt.T, conv_w, conv_b.reshape(1, conv_dim),
      dt_bias.reshape(1, heads), dt_bias.reshape(heads, 1))


def _ssd_scan_kernel(xs_ref, b_ref, c_ref, dt_ref, dtt_ref, alog_ref, alogt_ref, dskip_ref,
                     y_ref, h_ref, *, cps):
    ck = SSD_CHUNK
    hpg = h_ref.shape[2] // HEAD_DIM
    gw = hpg * HEAD_DIM

    @pl.when(pl.program_id(1) == 0)
    def _():
        h_ref[...] = jnp.zeros(h_ref.shape, F32)

    a_row = -jnp.exp(alog_ref[...])
    a_col = -jnp.exp(alogt_ref[...])
    ri = lax.broadcasted_iota(jnp.int32, (ck, ck), 0)
    ci = lax.broadcasted_iota(jnp.int32, (ck, ck), 1)
    causal = ci <= ri
    tril = causal.astype(F32)
    triu = (ri <= ci).astype(F32)
    lane = lax.broadcasted_iota(jnp.int32, (ck, gw), 1)
    lane1 = lax.broadcasted_iota(jnp.int32, (1, gw), 1)
    head_mask = [(lane >= j * HEAD_DIM) & (lane < (j + 1) * HEAD_DIM) for j in range(hpg)]

    def pick_head(vals, lanes):
        out = vals[hpg - 1]
        for j in range(hpg - 2, -1, -1):
            out = jnp.where(lanes < (j + 1) * HEAD_DIM, vals[j], out)
        return out

    for q in range(cps):
        rows = slice(q * ck, (q + 1) * ck)
        dt = dt_ref[rows, :]
        dtt = dtt_ref[:, rows]
        acum = jnp.dot(tril, dt * a_row, precision=lax.Precision.HIGHEST,
                       preferred_element_type=F32)
        acumt = jnp.dot(dtt * a_col, triu, precision=lax.Precision.HIGHEST,
                        preferred_element_type=F32)
        a_end_row = acum[ck - 1:ck, :]
        a_end_col = acumt[:, ck - 1:ck]
        wt = jnp.exp(a_end_col - acumt) * dtt
        e_end = jnp.exp(a_end_row)
        for g in range(SSD_GROUPS):
            gs = slice(g * SSD_STATE, (g + 1) * SSD_STATE)
            xs_gb = xs_ref[rows, g * gw:(g + 1) * gw]
            b_gb = b_ref[rows, gs]
            c_gb = c_ref[rows, gs]
            b_gt = b_gb.astype(F32).T
            cb = lax.dot_general(c_gb, b_gb, NT_DIMS, preferred_element_type=F32)
            h_g = h_ref[g]
            y_off = jnp.dot(c_gb, h_g.astype(BF16), preferred_element_type=F32)
            xs_heads = jnp.concatenate(
                [jnp.where(head_mask[j], xs_gb, jnp.zeros_like(xs_gb)) for j in range(hpg)], axis=0)
            m_parts, bw_parts, dec_parts = [], [], []
            e_start = None
            for j in range(hpg):
                hd = g * hpg + j
                col = jnp.broadcast_to(acum[:, hd:hd + 1], (ck, ck))
                row = acumt[hd:hd + 1, :]
                decay = jnp.exp(jnp.where(causal, col - row, -jnp.inf))
                m_parts.append((cb * decay * dtt[hd:hd + 1, :]).astype(BF16))
                bw_parts.append((b_gt * wt[hd:hd + 1, :]).astype(BF16))
                dec_parts.append(jnp.broadcast_to(e_end[:, hd:hd + 1], (1, gw)))
                e_col = jnp.exp(col)
                e_wide = jnp.concatenate([e_col] * (gw // ck), axis=1)
                e_start = e_wide if j == 0 else jnp.where(lane >= j * HEAD_DIM, e_wide, e_start)
            y_d = jnp.dot(jnp.concatenate(m_parts, axis=1), xs_heads,
                          preferred_element_type=F32)
            st = jnp.dot(jnp.concatenate(bw_parts, axis=1), xs_heads,
                         preferred_element_type=F32)
            y_g = y_d + y_off * e_start + dskip_ref[:, g * gw:(g + 1) * gw] * xs_gb.astype(F32)
            y_ref[rows, g * gw:(g + 1) * gw] = y_g.astype(y_ref.dtype)
            h_ref[g] = pick_head(dec_parts, lane1) * h_g + st


def ssd_scan(xbc, dt, dtt, a_log, d_skip, *, d_inner, cps=2):
    bz, s, _ = xbc.shape
    heads = a_log.shape[0]
    gn = SSD_GROUPS * SSD_STATE
    lt = cps * SSD_CHUNK
    xs_blocks = d_inner // gn
    return pl.pallas_call(
        functools.partial(_ssd_scan_kernel, cps=cps),
        out_shape=jax.ShapeDtypeStruct((bz, s, d_inner), BF16),
        grid=(bz, s // lt),
        in_specs=[pl.BlockSpec((None, lt, d_inner), lambda b, i: (b, i, 0)),
                  pl.BlockSpec((None, lt, gn), lambda b, i: (b, i, xs_blocks)),
                  pl.BlockSpec((None, lt, gn), lambda b, i: (b, i, xs_blocks + 1)),
                  pl.BlockSpec((None, lt, heads), lambda b, i: (b, i, 0)),
                  pl.BlockSpec((None, heads, lt), lambda b, i: (b, 0, i)),
                  _resident((1, heads)), _resident((heads, 1)), _resident((1, d_inner))],
        out_specs=pl.BlockSpec((None, lt, d_inner), lambda b, i: (b, i, 0)),
        scratch_shapes=[pltpu.VMEM((SSD_GROUPS, SSD_STATE, d_inner // SSD_GROUPS), F32)],
        compiler_params=_params(("parallel", "arbitrary")),
        name="ssd_scan",
    )(xbc, xbc, xbc, dt, dtt, a_log.reshape(1, heads), a_log.reshape(heads, 1),
      jnp.repeat(d_skip, HEAD_DIM).reshape(1, d_inner))


def _ssd_out_kernel(x_ref, y_ref, z_ref, nw_ref, w_ref, g_ref, b_ref, o_ref):
    d_inner = y_ref.shape[1]
    gw = d_inner // SSD_GROUPS
    x = x_ref[...]
    acc = jnp.zeros(x.shape, F32)
    for g in range(SSD_GROUPS):
        sl = slice(g * gw, (g + 1) * gw)
        gated = y_ref[:, sl].astype(F32) * _silu(z_ref[:, sl].astype(F32))
        ms = jnp.mean(gated * gated, axis=-1, keepdims=True)
        normed = (gated * lax.rsqrt(ms + EPS) * nw_ref[:, sl]).astype(BF16)
        acc = acc + jnp.dot(normed, w_ref[sl, :], preferred_element_type=F32)
    o_ref[...] = _layer_norm(ALPHA * x + acc, g_ref[...], b_ref[...])


def ssd_out(x, y, z, norm_w, w_out, g, b, *, tm=512):
    t, d = x.shape
    d_inner = y.shape[1]
    return pl.pallas_call(
        _ssd_out_kernel,
        out_shape=jax.ShapeDtypeStruct((t, d), F32),
        grid=(t // tm,),
        in_specs=[pl.BlockSpec((tm, d), lambda i: (i, 0)),
                  pl.BlockSpec((tm, d_inner), lambda i: (i, 0)),
                  pl.BlockSpec((tm, d_inner), lambda i: (i, 0)),
                  _resident((1, d_inner)), _resident(w_out.shape),
                  _resident((1, d)), _resident((1, d))],
        out_specs=pl.BlockSpec((tm, d), lambda i: (i, 0)),
        compiler_params=_params(("parallel",)),
        name="ssd_out",
    )(x, y, z, norm_w.reshape(1, d_inner), w_out, g.reshape(1, d), b.reshape(1, d))


def _attn_kernel(*refs, dilations):
    nb = len(dilations)
    n = ATTN_WINDOW
    q_refs = refs[0:nb]
    kc_refs, kp_refs = refs[nb:3 * nb:2], refs[nb + 1:3 * nb:2]
    vc_refs, vp_refs = refs[3 * nb:5 * nb:2], refs[3 * nb + 1:5 * nb:2]
    brev_ref, bfar_ref, o_ref = refs[5 * nb:5 * nb + 3]
    scratch = refs[5 * nb + 3:]
    kw_refs, vw_refs = scratch[0:nb], scratch[nb:2 * nb]
    bias_scr, o_scr, l_scr, lg_scr, mx_scr = scratch[2 * nb:]
    first_tile = pl.program_id(2) == 0

    ri = lax.broadcasted_iota(jnp.int32, (n, n), 0)
    ki = lax.broadcasted_iota(jnp.int32, (n, n), 1)
    low_q = lax.broadcasted_iota(jnp.int32, (n, LANES), 1) < HEAD_DIM
    low_v = lax.broadcasted_iota(jnp.int32, (2 * n, LANES), 1) < HEAD_DIM
    in_prev = lax.broadcasted_iota(jnp.int32, (1, 2 * n), 1) < n

    for g in range(nb):
        for e in range(2):
            circ = pltpu.roll(jnp.broadcast_to(brev_ref[g, e:e + 1, :], (n, n)), 0, 1,
                              stride=1, stride_axis=0)
            far = bfar_ref[g, e:e + 1, :]
            bias_scr[g, e, :, 0:n] = jnp.where(ki > ri, circ, jnp.where(ki == ri, far, NEG))
            bias_scr[g, e, :, n:2 * n] = jnp.where(ki <= ri, circ, NEG)

    for g in range(nb):
        kw_refs[g][:, 0:n, :] = kp_refs[g][...]
        kw_refs[g][:, n:, :] = kc_refs[g][...]
        vw_refs[g][:, 0:n, :] = vp_refs[g][...]
        vw_refs[g][:, n:, :] = vc_refs[g][...]

    one = jnp.ones((), BF16)
    zero = jnp.zeros((), BF16)
    units = ATTN_TILE // n
    for g, r in enumerate(dilations):
        blocks = units // r

        def split(idx, blocks=blocks):
            c, j = idx // blocks, idx % blocks
            return c, j, pl.multiple_of(j * n, n)

        def logits_pass(idx, carry, g=g, split=split):
            c, j, start = split(idx)
            q_u = q_refs[g][c, pl.ds(start, n), :]
            k_w = kw_refs[g][c, pl.ds(start, 2 * n), :]
            no_prev = jnp.logical_and(first_tile, j == 0)
            neg_row = jnp.where(in_prev, jnp.where(no_prev, NEG, 0.0), 0.0)
            for e in range(2):
                own_q = low_q if e == 0 else ~low_q
                logits = lax.dot_general(jnp.where(own_q, q_u, zero), k_w, NT_DIMS,
                                         preferred_element_type=F32)
                logits = logits + bias_scr[g, e] + neg_row
                lg_scr[idx, e] = logits
                mx_scr[idx, e] = jnp.broadcast_to(jnp.max(logits, axis=-1, keepdims=True),
                                                  (n, LANES))
            return carry

        def value_pass(idx, carry, g=g, r=r, split=split):
            c, j, start = split(idx)
            v_w = vw_refs[g][c, pl.ds(start, 2 * n), :]
            res = []
            for e in range(2):
                m = mx_scr[idx, e]
                p = jnp.concatenate(
                    [jnp.exp(lg_scr[idx, e, :, half * n:(half + 1) * n] - m) for half in range(2)],
                    axis=-1).astype(BF16)
                res.append(jnp.dot(p, jnp.where(low_v if e == 0 else ~low_v, v_w, one),
                                   preferred_element_type=F32))
            num = jnp.where(low_q, res[0], res[1])
            den = pltpu.roll(jnp.where(low_q, res[1], res[0]), HEAD_DIM, 1)
            rows = pl.ds(j * (n * r) + c, n, stride=r) if r > 1 else pl.ds(start, n)
            o_scr[g, rows, :] = num / den
            l_scr[g, rows, :] = jnp.where(low_q, mx_scr[idx, 0], mx_scr[idx, 1]) + jnp.log(den)
            return carry

        lax.fori_loop(0, units, logits_pass, 0, unroll=8)
        lax.fori_loop(0, units, value_pass, 0, unroll=8)

    rb = 2 * n

    def mix(tb, carry):
        rows = pl.ds(pl.multiple_of(tb * rb, rb), rb)
        ls = [l_scr[g, rows, :] for g in range(nb)]
        top = functools.reduce(jnp.maximum, ls)
        ws = [jnp.exp(l - top) for l in ls]
        acc = functools.reduce(lambda a, b: a + b, [w * o_scr[g, rows, :] for g, w in enumerate(ws)])
        o_ref[rows, :] = (acc / functools.reduce(lambda a, b: a + b, ws)).astype(o_ref.dtype)
        return carry

    lax.fori_loop(0, ATTN_TILE // rb, mix, 0)


def attn_mix(qs, ks, vs, brev, bfar, *, dilations):
    bz, groups, _, s, _ = qs[0].shape
    n, tile = ATTN_WINDOW, ATTN_TILE
    nb = len(dilations)

    def cur(r):
        return pl.BlockSpec((None, None, r, tile // r, LANES), lambda b, hg, i: (b, hg, 0, i, 0))

    def prev(r):
        per_tile = tile // r // n
        return pl.BlockSpec((None, None, r, n, LANES),
                            lambda b, hg, i: (b, hg, 0, jnp.maximum(i * per_tile - 1, 0), 0))

    in_specs = [cur(r) for r in dilations]
    args = list(qs)
    for arrs in (ks, vs):
        for a, r in zip(arrs, dilations):
            in_specs += [cur(r), prev(r)]
            args += [a, a]
    in_specs += [pl.BlockSpec((nb, None, 2, LANES), lambda b, hg, i: (0, hg, 0, 0)),
                 pl.BlockSpec((nb, None, 2, 1), lambda b, hg, i: (0, hg, 0, 0))]
    args += [brev, bfar]
    windows = [pltpu.VMEM((r, n + tile // r, LANES), BF16) for r in dilations]
    return pl.pallas_call(
        functools.partial(_attn_kernel, dilations=dilations),
        out_shape=jax.ShapeDtypeStruct((bz, groups, s, LANES), BF16),
        grid=(bz, groups, s // tile),
        in_specs=in_specs,
        out_specs=pl.BlockSpec((None, None, tile, LANES), lambda b, hg, i: (b, hg, i, 0)),
        scratch_shapes=windows + windows + [pltpu.VMEM((nb, 2, n, 2 * n), F32),
                                            pltpu.VMEM((nb, tile, LANES), F32),
                                            pltpu.VMEM((nb, tile, LANES), F32),
                                            pltpu.VMEM((tile // n, 2, n, 2 * n), F32),
                                            pltpu.VMEM((tile // n, 2, n, LANES), F32)],
        compiler_params=_params(("parallel", "parallel", "arbitrary")),
        name="attn_mix",
    )(*args)


def _attn_out_kernel(x_ref, o_ref, w_ref, g_ref, b_ref, out_ref):
    mix = jnp.concatenate([o_ref[lg] for lg in range(o_ref.shape[0])], axis=-1)
    h = jnp.dot(mix, w_ref[...], preferred_element_type=F32)
    out_ref[...] = _layer_norm(ALPHA * x_ref[...] + h, g_ref[...], b_ref[...])


def attn_out(x, o, w_o, g, b, *, tm=512):
    bz, s, d = x.shape
    groups = o.shape[1]
    return pl.pallas_call(
        _attn_out_kernel,
        out_shape=jax.ShapeDtypeStruct((bz, s, d), F32),
        grid=(bz, s // tm),
        in_specs=[pl.BlockSpec((None, tm, d), lambda b, i: (b, i, 0)),
                  pl.BlockSpec((None, groups, tm, LANES), lambda b, i: (b, 0, i, 0)),
                  _resident(w_o.shape), _resident((1, d)), _resident((1, d))],
        out_specs=pl.BlockSpec((None, tm, d), lambda b, i: (b, i, 0)),
        compiler_params=_params(("parallel", "parallel")),
        name="attn_out",
    )(x, o, w_o, g.reshape(1, d), b.reshape(1, d))


def _t5_causal_bucket(dist):
    max_exact = NUM_BUCKETS // 2
    logv = (jnp.log(jnp.maximum(dist, 1).astype(F32) / max_exact)
            / math.log(MAX_DISTANCE / max_exact))
    large = jnp.minimum(max_exact + (logv * (NUM_BUCKETS - max_exact)).astype(jnp.int32),
                        NUM_BUCKETS - 1)
    return jnp.where(dist < max_exact, dist, large)


def _bias_rows(rel_bias, heads):
    n = ATTN_WINDOW
    rev, far = [], []
    for branch, (window, r) in enumerate(ATTN_BRANCHES):
        buckets = _t5_causal_bucket(jnp.arange(n + 1, dtype=jnp.int32) * r)
        per_dist = rel_bias[buckets][:, branch * heads:(branch + 1) * heads].T.astype(F32)
        rev.append(per_dist[:, (-jnp.arange(n)) % n])
        far.append(per_dist[:, n:n + 1])
    return (jnp.stack(rev).reshape(len(rev), heads // 2, 2, n),
            jnp.stack(far).reshape(len(far), heads // 2, 2, 1))


def kernel(x, ln_g, ln_b, ffn_w_in, ffn_w_out, m_in_proj, m_conv_w, m_conv_b, m_dt_bias, m_a_log,
           m_d, m_norm_w, m_out_proj, a_w_q, a_w_o, kv_w, rel_bias):
    bz, s, d = x.shape
    t = bz * s
    d_inner = m_out_proj.shape[1]
    conv_dim = m_conv_w.shape[2]
    width = a_w_o.shape[1]
    heads = width // HEAD_DIM
    dilations = tuple(r for _, r in ATTN_BRANCHES)
    nb = len(dilations)
    assert all(window // r == ATTN_WINDOW for window, r in ATTN_BRANCHES)
    x = x.reshape(t, d)
    ks = vs = brev = bfar = None
    for layer in range(DEPTH):
        x = ffn_ln(x, ffn_w_in[layer, 0].astype(BF16), ffn_w_out[layer, 0].astype(BF16),
                   ln_g[layer, 0], ln_b[layer, 0])
        if layer < N_SSD_LAYERS:
            w_in = m_in_proj[layer].astype(BF16)
            z, xbc, dt, dtt = ssd_in(x.reshape(bz, s, d), w_in[:, :d_inner],
                                     w_in[:, d_inner:d_inner + conv_dim],
                                     w_in[:, d_inner + conv_dim:], m_conv_w[layer],
                                     m_conv_b[layer], m_dt_bias[layer])
            y = ssd_scan(xbc, dt, dtt, m_a_log[layer], m_d[layer], d_inner=d_inner)
            x = ssd_out(x, y.reshape(t, d_inner), z.reshape(t, d_inner), m_norm_w[layer],
                        m_out_proj[layer].astype(BF16), ln_g[layer, 1], ln_b[layer, 1])
        else:
            a = layer - N_SSD_LAYERS
            qs = proj_cm(x.reshape(bz, s, d), a_w_q[a].astype(BF16), dilations=dilations,
                         width=width, scale=HEAD_DIM ** -0.5)
            o = attn_mix(qs, ks, vs, brev, bfar, dilations=dilations)
            x = attn_out(x.reshape(bz, s, d), o, a_w_o[a].astype(BF16), ln_g[layer, 1],
                         ln_b[layer, 1]).reshape(t, d)
        x = ffn_ln(x, ffn_w_in[layer, 1].astype(BF16), ffn_w_out[layer, 1].astype(BF16),
                   ln_g[layer, 2], ln_b[layer, 2])
        if layer == N_SSD_LAYERS - 1:
            kvs = proj_cm(x.reshape(bz, s, d), kv_w.astype(BF16), dilations=dilations * 2,
                          width=width)
            ks, vs = kvs[:nb], kvs[nb:]
            brev, bfar = _bias_rows(rel_bias, heads)
    return x.reshape(bz, s, d)
```

```python
import functools
import math

import jax
import jax.numpy as jnp
from jax import lax
from jax.experimental import pallas as pl
from jax.experimental.pallas import tpu as pltpu

F32 = jnp.float32
BF16 = jnp.bfloat16

DEPTH = 4
N_SSD_LAYERS = DEPTH // 2
ALPHA = (2.0 * DEPTH) ** 0.25
EPS = 1e-5
FFN_RES = 0.5
HEAD_DIM = 64
SSD_GROUPS = 8
SSD_STATE = 128
SSD_CHUNK = 128
SSD_CONV = 4
ATTN_BRANCHES = ((128, 1), (512, 4), (2048, 16))
ATTN_WINDOW = 128
ATTN_TILE = ATTN_WINDOW * max(r for _, r in ATTN_BRANCHES)
NUM_BUCKETS = 32
MAX_DISTANCE = 2048
NEG = -1e30
LOG2_E = math.log2(math.e)

LANES = 128
SUBLANES = 8
VMEM_LIMIT_BYTES = 56 * 1024 * 1024

NT_DIMS = (((1,), (1,)), ((), ()))


def _params(semantics):
    return pltpu.CompilerParams(dimension_semantics=semantics,
                                vmem_limit_bytes=VMEM_LIMIT_BYTES)


def _resident(shape):
    zeros = (0,) * len(shape)
    return pl.BlockSpec(shape, lambda *_: zeros, pipeline_mode=pl.Buffered(1))


def _layer_norm(y, g, b):
    mu = jnp.mean(y, axis=-1, keepdims=True)
    d = y - mu
    var = jnp.mean(d * d, axis=-1, keepdims=True)
    return d * lax.rsqrt(var + EPS) * g + b


def _silu(v):
    return v * jax.nn.sigmoid(v)


def _ffn_kernel(x_ref, win_ref, wout_ref, g_ref, b_ref, o_ref, *, d_ff, ck, sub):
    for h in range(x_ref.shape[0] // sub):
        rows = slice(h * sub, (h + 1) * sub)
        x = x_ref[rows, :]
        xb = x.astype(BF16)
        acc = jnp.zeros(x.shape, F32)
        for c in range(d_ff // ck):
            gate = jnp.dot(xb, win_ref[:, c * ck:(c + 1) * ck], preferred_element_type=F32)
            up = jnp.dot(xb, win_ref[:, d_ff + c * ck:d_ff + (c + 1) * ck],
                         preferred_element_type=F32)
            act = (_silu(gate) * up).astype(BF16)
            acc = acc + jnp.dot(act, wout_ref[c * ck:(c + 1) * ck, :],
                                preferred_element_type=F32)
        o_ref[rows, :] = _layer_norm(ALPHA * x + FFN_RES * acc, g_ref[...], b_ref[...])


def ffn_ln(x, w_in, w_out, g, b, *, tm=1024, ck=256, sub=512):
    t, d = x.shape
    d_ff = w_out.shape[0]
    return pl.pallas_call(
        functools.partial(_ffn_kernel, d_ff=d_ff, ck=ck, sub=sub),
        out_shape=jax.ShapeDtypeStruct((t, d), F32),
        grid=(t // tm,),
        in_specs=[pl.BlockSpec((tm, d), lambda i: (i, 0)),
                  _resident(w_in.shape), _resident(w_out.shape),
                  _resident((1, d)), _resident((1, d))],
        out_specs=pl.BlockSpec((tm, d), lambda i: (i, 0)),
        compiler_params=_params(("parallel",)),
        name="ffn_ln",
    )(x, w_in, w_out, g.reshape(1, d), b.reshape(1, d))


def _proj_cm_kernel(x_ref, w_ref, *refs, dilations, width, scale):
    o_refs, scr = refs[:-1], refs[-1]
    xb = x_ref[...].astype(BF16)
    tm = xb.shape[0]
    for n, (o_ref, r) in enumerate(zip(o_refs, dilations)):
        y = jnp.dot(xb, w_ref[:, n * width:(n + 1) * width], preferred_element_type=F32) * scale
        if r == 1:
            for lg in range(width // LANES):
                o_ref[lg, 0] = y[:, lg * LANES:(lg + 1) * LANES].astype(BF16)
        else:
            for lg in range(width // LANES):
                scr[lg] = y[:, lg * LANES:(lg + 1) * LANES]
            for lg in range(width // LANES):
                for c in range(r):
                    o_ref[lg, c] = scr[lg, pl.ds(c, tm // r, stride=r), :].astype(BF16)


def proj_cm(x, w, *, dilations, width, scale=1.0, tm=512):
    bz, s, d = x.shape
    groups = width // LANES
    kern = functools.partial(_proj_cm_kernel, dilations=dilations, width=width, scale=scale)
    return pl.pallas_call(
        kern,
        out_shape=tuple(jax.ShapeDtypeStruct((bz, groups, r, s // r, LANES), BF16)
                        for r in dilations),
        grid=(bz, s // tm),
        in_specs=[pl.BlockSpec((None, tm, d), lambda b, i: (b, i, 0)), _resident(w.shape)],
        out_specs=tuple(pl.BlockSpec((None, groups, r, tm // r, LANES),
                                     lambda b, i: (b, 0, 0, i, 0)) for r in dilations),
        scratch_shapes=[pltpu.VMEM((groups, tm, LANES), F32)],
        compiler_params=_params(("parallel", "parallel")),
        name="proj_cm",
    )(x, w)


def _ssd_in_kernel(x_ref, win_ref, wdtt_ref, cw_ref, cb_ref, dtb_ref, dtbt_ref,
                   z_ref, xbc_ref, dt_ref, dtt_ref, halo_ref, stage_ref, *, tm, cn, d_inner, conv_dim):
    i = pl.program_id(1)
    xb = x_ref[...].astype(BF16)
    hist = SUBLANES
    per = tm // SUBLANES
    gpc = cn // LANES

    @pl.when(i == 0)
    def _():
        halo_ref[:, 0:hist, :] = jnp.zeros((halo_ref.shape[0], hist, LANES), F32)

    @pl.when(i > 0)
    def _():
        halo_ref[:, 0:hist, :] = halo_ref[:, tm:tm + hist, :]

    for c in range(d_inner // cn):
        sl = slice(c * cn, (c + 1) * cn)
        z_ref[:, sl] = jnp.dot(xb, win_ref[:, sl], preferred_element_type=F32).astype(z_ref.dtype)
    for c in range(conv_dim // cn):
        u = jnp.dot(xb, win_ref[:, d_inner + c * cn:d_inner + (c + 1) * cn],
                    preferred_element_type=F32)
        for lg in range(gpc):
            halo_ref[c * gpc + lg, hist:hist + tm, :] = u[:, lg * LANES:(lg + 1) * LANES]
        for lg in range(gpc):
            grp = c * gpc + lg
            cols = slice(grp * LANES, (grp + 1) * LANES)
            taps = {back: halo_ref[grp, pl.ds(hist - back, per, stride=SUBLANES), :]
                    for back in range(1 - SUBLANES, SSD_CONV)}
            for p in range(SUBLANES):
                conv = cb_ref[:, cols] + jnp.zeros((per, LANES), F32)
                for k in range(SSD_CONV):
                    conv = conv + cw_ref[k:k + 1, cols] * taps[SSD_CONV - 1 - k - p]
                stage_ref[lg, pl.ds(p, per, stride=SUBLANES), :] = _silu(conv)
            xbc_ref[:, cols] = stage_ref[lg].astype(xbc_ref.dtype)
    dt_raw = jnp.dot(xb, win_ref[:, d_inner + conv_dim:], preferred_element_type=F32)
    dt_ref[...] = jax.nn.softplus(dt_raw + dtb_ref[...])
    dtt_raw = lax.dot_general(wdtt_ref[...], xb, NT_DIMS, preferred_element_type=F32)
    dtt_ref[...] = jax.nn.softplus(dtt_raw + dtbt_ref[...])


def ssd_in(x, w_in, conv_w, conv_b, dt_bias, *, d_inner, tm=256, cn=512):
    bz, s, d = x.shape
    conv_dim, heads = conv_w.shape[1], dt_bias.shape[0]
    kern = functools.partial(_ssd_in_kernel, tm=tm, cn=cn, d_inner=d_inner, conv_dim=conv_dim)
    return pl.pallas_call(
        kern,
        out_shape=(jax.ShapeDtypeStruct((bz, s, d_inner), BF16),
                   jax.ShapeDtypeStruct((bz, s, conv_dim), BF16),
                   jax.ShapeDtypeStruct((bz, s, heads), F32),
                   jax.ShapeDtypeStruct((bz, heads, s), F32)),
        grid=(bz, s // tm),
        in_specs=[pl.BlockSpec((None, tm, d), lambda b, i: (b, i, 0)),
                  _resident(w_in.shape), _resident((heads, d)), _resident(conv_w.shape),
                  _resident((1, conv_dim)), _resident((1, heads)), _resident((heads, 1))],
        out_specs=(pl.BlockSpec((None, tm, d_inner), lambda b, i: (b, i, 0)),
                   pl.BlockSpec((None, tm, conv_dim), lambda b, i: (b, i, 0)),
                   pl.BlockSpec((None, tm, heads), lambda b, i: (b, i, 0)),
                   pl.BlockSpec((None, heads, tm), lambda b, i: (b, 0, i))),
        scratch_shapes=[pltpu.VMEM((conv_dim // LANES, tm + SUBLANES, LANES), F32),
                        pltpu.VMEM((cn // LANES, tm, LANES), F32)],
        compiler_params=_params(("parallel", "arbitrary")),
        name="ssd_in",
    )(x, w_in, w_in[:, d_inner + conv_dim:].T, conv_w, conv_b.reshape(1, conv_dim),
      dt_bias.reshape(1, heads), dt_bias.reshape(heads, 1))


def _ssd_scan_kernel(xs_ref, b_ref, c_ref, dt_ref, dtt_ref, alog_ref, alogt_ref, dskip_ref,
                     y_ref, h_ref, *, cps):
    ck = SSD_CHUNK
    hpg = h_ref.shape[2] // HEAD_DIM
    gw = hpg * HEAD_DIM

    @pl.when(pl.program_id(1) == 0)
    def _():
        h_ref[...] = jnp.zeros(h_ref.shape, F32)

    a_row = -jnp.exp(alog_ref[...])
    a_col = -jnp.exp(alogt_ref[...])
    ri = lax.broadcasted_iota(jnp.int32, (ck, ck), 0)
    ci = lax.broadcasted_iota(jnp.int32, (ck, ck), 1)
    causal = ci <= ri
    tril = causal.astype(F32)
    triu = (ri <= ci).astype(F32)
    lane = lax.broadcasted_iota(jnp.int32, (ck, gw), 1)
    lane1 = lax.broadcasted_iota(jnp.int32, (1, gw), 1)
    head_mask = [(lane >= j * HEAD_DIM) & (lane < (j + 1) * HEAD_DIM) for j in range(hpg)]

    def pick_head(vals, lanes):
        out = vals[hpg - 1]
        for j in range(hpg - 2, -1, -1):
            out = jnp.where(lanes < (j + 1) * HEAD_DIM, vals[j], out)
        return out

    for q in range(cps):
        rows = slice(q * ck, (q + 1) * ck)
        dt = dt_ref[rows, :]
        dtt = dtt_ref[:, rows]
        acum = jnp.dot(tril, dt * a_row, precision=lax.Precision.HIGHEST,
                       preferred_element_type=F32)
        acumt = jnp.dot(dtt * a_col, triu, precision=lax.Precision.HIGHEST,
                        preferred_element_type=F32)
        a_end_row = acum[ck - 1:ck, :]
        a_end_col = acumt[:, ck - 1:ck]
        wt = jnp.exp(a_end_col - acumt) * dtt
        e_end = jnp.exp(a_end_row)
        for g in range(SSD_GROUPS):
            gs = slice(g * SSD_STATE, (g + 1) * SSD_STATE)
            xs_gb = xs_ref[rows, g * gw:(g + 1) * gw]
            b_gb = b_ref[rows, gs]
            c_gb = c_ref[rows, gs]
            b_gt = b_gb.astype(F32).T
            cb = lax.dot_general(c_gb, b_gb, NT_DIMS, preferred_element_type=F32)
            h_g = h_ref[g]
            y_off = jnp.dot(c_gb, h_g.astype(BF16), preferred_element_type=F32)
            xs_heads = jnp.concatenate(
                [jnp.where(head_mask[j], xs_gb, jnp.zeros_like(xs_gb)) for j in range(hpg)], axis=0)
            m_parts, bw_parts, dec_parts = [], [], []
            e_start = None
            for j in range(hpg):
                hd = g * hpg + j
                col = jnp.broadcast_to(acum[:, hd:hd + 1], (ck, ck))
                row = acumt[hd:hd + 1, :]
                decay = jnp.exp(jnp.where(causal, col - row, -jnp.inf))
                m_parts.append((cb * decay * dtt[hd:hd + 1, :]).astype(BF16))
                bw_parts.append((b_gt * wt[hd:hd + 1, :]).astype(BF16))
                dec_parts.append(jnp.broadcast_to(e_end[:, hd:hd + 1], (1, gw)))
                e_col = jnp.exp(col)
                e_wide = jnp.concatenate([e_col] * (gw // ck), axis=1)
                e_start = e_wide if j == 0 else jnp.where(lane >= j * HEAD_DIM, e_wide, e_start)
            y_d = jnp.dot(jnp.concatenate(m_parts, axis=1), xs_heads,
                          preferred_element_type=F32)
            st = jnp.dot(jnp.concatenate(bw_parts, axis=1), xs_heads,
                         preferred_element_type=F32)
            y_g = y_d + y_off * e_start + dskip_ref[:, g * gw:(g + 1) * gw] * xs_gb.astype(F32)
            y_ref[rows, g * gw:(g + 1) * gw] = y_g.astype(y_ref.dtype)
            h_ref[g] = pick_head(dec_parts, lane1) * h_g + st


def ssd_scan(xbc, dt, dtt, a_log, d_skip, *, d_inner, cps=2):
    bz, s, _ = xbc.shape
    heads = a_log.shape[0]
    gn = SSD_GROUPS * SSD_STATE
    lt = cps * SSD_CHUNK
    xs_blocks = d_inner // gn
    return pl.pallas_call(
        functools.partial(_ssd_scan_kernel, cps=cps),
        out_shape=jax.ShapeDtypeStruct((bz, s, d_inner), BF16),
        grid=(bz, s // lt),
        in_specs=[pl.BlockSpec((None, lt, d_inner), lambda b, i: (b, i, 0)),
                  pl.BlockSpec((None, lt, gn), lambda b, i: (b, i, xs_blocks)),
                  pl.BlockSpec((None, lt, gn), lambda b, i: (b, i, xs_blocks + 1)),
                  pl.BlockSpec((None, lt, heads), lambda b, i: (b, i, 0)),
                  pl.BlockSpec((None, heads, lt), lambda b, i: (b, 0, i)),
                  _resident((1, heads)), _resident((heads, 1)), _resident((1, d_inner))],
        out_specs=pl.BlockSpec((None, lt, d_inner), lambda b, i: (b, i, 0)),
        scratch_shapes=[pltpu.VMEM((SSD_GROUPS, SSD_STATE, d_inner // SSD_GROUPS), F32)],
        compiler_params=_params(("parallel", "arbitrary")),
        name="ssd_scan",
    )(xbc, xbc, xbc, dt, dtt, a_log.reshape(1, heads), a_log.reshape(heads, 1),
      jnp.repeat(d_skip, HEAD_DIM).reshape(1, d_inner))


def _ssd_out_kernel(x_ref, y_ref, z_ref, nw_ref, w_ref, g_ref, b_ref, o_ref):
    d_inner = y_ref.shape[1]
    gw = d_inner // SSD_GROUPS
    x = x_ref[...]
    acc = jnp.zeros(x.shape, F32)
    for g in range(SSD_GROUPS):
        sl = slice(g * gw, (g + 1) * gw)
        gated = y_ref[:, sl].astype(F32) * _silu(z_ref[:, sl].astype(F32))
        ms = jnp.mean(gated * gated, axis=-1, keepdims=True)
        normed = (gated * lax.rsqrt(ms + EPS) * nw_ref[:, sl]).astype(BF16)
        acc = acc + jnp.dot(normed, w_ref[sl, :], preferred_element_type=F32)
    o_ref[...] = _layer_norm(ALPHA * x + acc, g_ref[...], b_ref[...])


def ssd_out(x, y, z, norm_w, w_out, g, b, *, tm=512):
    t, d = x.shape
    d_inner = y.shape[1]
    return pl.pallas_call(
        _ssd_out_kernel,
        out_shape=jax.ShapeDtypeStruct((t, d), F32),
        grid=(t // tm,),
        in_specs=[pl.BlockSpec((tm, d), lambda i: (i, 0)),
                  pl.BlockSpec((tm, d_inner), lambda i: (i, 0)),
                  pl.BlockSpec((tm, d_inner), lambda i: (i, 0)),
                  _resident((1, d_inner)), _resident(w_out.shape),
                  _resident((1, d)), _resident((1, d))],
        out_specs=pl.BlockSpec((tm, d), lambda i: (i, 0)),
        compiler_params=_params(("parallel",)),
        name="ssd_out",
    )(x, y, z, norm_w.reshape(1, d_inner), w_out, g.reshape(1, d), b.reshape(1, d))


def _attn_kernel(*refs, dilations):
    nb = len(dilations)
    n = ATTN_WINDOW
    q_refs = refs[0:nb]
    kc_refs, kp_refs = refs[nb:3 * nb:2], refs[nb + 1:3 * nb:2]
    vc_refs, vp_refs = refs[3 * nb:5 * nb:2], refs[3 * nb + 1:5 * nb:2]
    brev_ref, bfar_ref, o_ref = refs[5 * nb:5 * nb + 3]
    scratch = refs[5 * nb + 3:]
    kw_refs, vw_refs = scratch[0:nb], scratch[nb:2 * nb]
    bias_scr, o_scr, l_scr, lg_scr, mx_scr = scratch[2 * nb:]
    first_tile = pl.program_id(2) == 0

    ri = lax.broadcasted_iota(jnp.int32, (n, n), 0)
    ki = lax.broadcasted_iota(jnp.int32, (n, n), 1)
    low_q = lax.broadcasted_iota(jnp.int32, (n, LANES), 1) < HEAD_DIM
    low_v = lax.broadcasted_iota(jnp.int32, (2 * n, LANES), 1) < HEAD_DIM
    in_prev = lax.broadcasted_iota(jnp.int32, (1, 2 * n), 1) < n

    for g in range(nb):
        for e in range(2):
            circ = pltpu.roll(jnp.broadcast_to(brev_ref[g, e:e + 1, :], (n, n)), 0, 1,
                              stride=1, stride_axis=0)
            far = bfar_ref[g, e:e + 1, :]
            bias_scr[g, e, :, 0:n] = jnp.where(ki > ri, circ, jnp.where(ki == ri, far, NEG))
            bias_scr[g, e, :, n:2 * n] = jnp.where(ki <= ri, circ, NEG)

    for g in range(nb):
        kw_refs[g][:, 0:n, :] = kp_refs[g][...]
        kw_refs[g][:, n:, :] = kc_refs[g][...]
        vw_refs[g][:, 0:n, :] = vp_refs[g][...]
        vw_refs[g][:, n:, :] = vc_refs[g][...]

    one = jnp.ones((), BF16)
    zero = jnp.zeros((), BF16)
    units = ATTN_TILE // n

    def split(g, idx):
        blocks = units // dilations[g]
        c, j = idx // blocks, idx % blocks
        return c, j, pl.multiple_of(j * n, n)

    def logits_stage(g, idx):
        c, j, start = split(g, idx)
        q_u = q_refs[g][c, pl.ds(start, n), :]
        k_w = kw_refs[g][c, pl.ds(start, 2 * n), :]
        no_prev = jnp.logical_and(first_tile, j == 0)
        neg_row = jnp.where(in_prev, jnp.where(no_prev, NEG, 0.0), 0.0)
        for e in range(2):
            own_q = low_q if e == 0 else ~low_q
            logits = lax.dot_general(jnp.where(own_q, q_u, zero), k_w, NT_DIMS,
                                     preferred_element_type=F32)
            logits = logits + bias_scr[g, e] + neg_row
            lg_scr[g % 2, idx, e] = logits
            mx_scr[g % 2, idx, e] = jnp.broadcast_to(jnp.max(logits, axis=-1, keepdims=True),
                                                     (n, LANES))

    def value_stage(g, idx):
        r = dilations[g]
        c, j, start = split(g, idx)
        v_w = vw_refs[g][c, pl.ds(start, 2 * n), :]
        res = []
        for e in range(2):
            m = mx_scr[g % 2, idx, e]
            p = jnp.concatenate(
                [jnp.exp2(lg_scr[g % 2, idx, e, :, half * n:(half + 1) * n] - m)
                 for half in range(2)], axis=-1).astype(BF16)
            res.append(jnp.dot(p, jnp.where(low_v if e == 0 else ~low_v, v_w, one),
                               preferred_element_type=F32))
        num = jnp.where(low_q, res[0], res[1])
        den = pltpu.roll(jnp.where(low_q, res[1], res[0]), HEAD_DIM, 1)
        rows = pl.ds(j * (n * r) + c, n, stride=r) if r > 1 else pl.ds(start, n)
        o_scr[g, rows, :] = num / den
        l_scr[g, rows, :] = (jnp.where(low_q, mx_scr[g % 2, idx, 0], mx_scr[g % 2, idx, 1])
                             + jnp.log2(den))

    for t in range(nb + 1):
        def body(idx, carry, t=t):
            if t > 0:
                value_stage(t - 1, idx)
            if t < nb:
                logits_stage(t, idx)
            return carry

        lax.fori_loop(0, units, body, 0, unroll=8)

    rb = 2 * n

    def mix(tb, carry):
        rows = pl.ds(pl.multiple_of(tb * rb, rb), rb)
        ls = [l_scr[g, rows, :] for g in range(nb)]
        top = functools.reduce(jnp.maximum, ls)
        ws = [jnp.exp2(l - top) for l in ls]
        acc = functools.reduce(lambda a, b: a + b, [w * o_scr[g, rows, :] for g, w in enumerate(ws)])
        o_ref[rows, :] = (acc / functools.reduce(lambda a, b: a + b, ws)).astype(o_ref.dtype)
        return carry

    lax.fori_loop(0, ATTN_TILE // rb, mix, 0)


def attn_mix(qs, ks, vs, brev, bfar, *, dilations):
    bz, groups, _, s, _ = qs[0].shape
    n, tile = ATTN_WINDOW, ATTN_TILE
    nb = len(dilations)

    def cur(r):
        return pl.BlockSpec((None, None, r, tile // r, LANES), lambda b, hg, i: (b, hg, 0, i, 0))

    def prev(r):
        per_tile = tile // r // n
        return pl.BlockSpec((None, None, r, n, LANES),
                            lambda b, hg, i: (b, hg, 0, jnp.maximum(i * per_tile - 1, 0), 0))

    in_specs = [cur(r) for r in dilations]
    args = list(qs)
    for arrs in (ks, vs):
        for a, r in zip(arrs, dilations):
            in_specs += [cur(r), prev(r)]
            args += [a, a]
    in_specs += [pl.BlockSpec((nb, None, 2, LANES), lambda b, hg, i: (0, hg, 0, 0)),
                 pl.BlockSpec((nb, None, 2, 1), lambda b, hg, i: (0, hg, 0, 0))]
    args += [brev, bfar]
    windows = [pltpu.VMEM((r, n + tile // r, LANES), BF16) for r in dilations]
    return pl.pallas_call(
        functools.partial(_attn_kernel, dilations=dilations),
        out_shape=jax.ShapeDtypeStruct((bz, groups, s, LANES), BF16),
        grid=(bz, groups, s // tile),
        in_specs=in_specs,
        out_specs=pl.BlockSpec((None, None, tile, LANES), lambda b, hg, i: (b, hg, i, 0)),
        scratch_shapes=windows + windows + [pltpu.VMEM((nb, 2, n, 2 * n), F32),
                                            pltpu.VMEM((nb, tile, LANES), F32),
                                            pltpu.VMEM((nb, tile, LANES), F32),
                                            pltpu.VMEM((2, tile // n, 2, n, 2 * n), F32),
                                            pltpu.VMEM((2, tile // n, 2, n, LANES), F32)],
        compiler_params=_params(("parallel", "parallel", "arbitrary")),
        name="attn_mix",
    )(*args)


def _attn_out_kernel(x_ref, o_ref, w_ref, g_ref, b_ref, out_ref):
    mix = jnp.concatenate([o_ref[lg] for lg in range(o_ref.shape[0])], axis=-1)
    h = jnp.dot(mix, w_ref[...], preferred_element_type=F32)
    out_ref[...] = _layer_norm(ALPHA * x_ref[...] + h, g_ref[...], b_ref[...])


def attn_out(x, o, w_o, g, b, *, tm=512):
    bz, s, d = x.shape
    groups = o.shape[1]
    return pl.pallas_call(
        _attn_out_kernel,
        out_shape=jax.ShapeDtypeStruct((bz, s, d), F32),
        grid=(bz, s // tm),
        in_specs=[pl.BlockSpec((None, tm, d), lambda b, i: (b, i, 0)),
                  pl.BlockSpec((None, groups, tm, LANES), lambda b, i: (b, 0, i, 0)),
                  _resident(w_o.shape), _resident((1, d)), _resident((1, d))],
        out_specs=pl.BlockSpec((None, tm, d), lambda b, i: (b, i, 0)),
        compiler_params=_params(("parallel", "parallel")),
        name="attn_out",
    )(x, o, w_o, g.reshape(1, d), b.reshape(1, d))


def _t5_causal_bucket(dist):
    max_exact = NUM_BUCKETS // 2
    logv = (jnp.log(jnp.maximum(dist, 1).astype(F32) / max_exact)
            / math.log(MAX_DISTANCE / max_exact))
    large = jnp.minimum(max_exact + (logv * (NUM_BUCKETS - max_exact)).astype(jnp.int32),
                        NUM_BUCKETS - 1)
    return jnp.where(dist < max_exact, dist, large)


def _bias_rows(rel_bias, heads):
    n = ATTN_WINDOW
    rev, far = [], []
    for branch, (window, r) in enumerate(ATTN_BRANCHES):
        buckets = _t5_causal_bucket(jnp.arange(n + 1, dtype=jnp.int32) * r)
        per_dist = rel_bias[buckets][:, branch * heads:(branch + 1) * heads].T.astype(F32) * LOG2_E
        rev.append(per_dist[:, (-jnp.arange(n)) % n])
        far.append(per_dist[:, n:n + 1])
    return (jnp.stack(rev).reshape(len(rev), heads // 2, 2, n),
            jnp.stack(far).reshape(len(far), heads // 2, 2, 1))


def kernel(x, ln_g, ln_b, ffn_w_in, ffn_w_out, m_in_proj, m_conv_w, m_conv_b, m_dt_bias, m_a_log,
           m_d, m_norm_w, m_out_proj, a_w_q, a_w_o, kv_w, rel_bias):
    bz, s, d = x.shape
    t = bz * s
    d_inner = m_out_proj.shape[1]
    conv_dim = m_conv_w.shape[2]
    width = a_w_o.shape[1]
    heads = width // HEAD_DIM
    dilations = tuple(r for _, r in ATTN_BRANCHES)
    nb = len(dilations)
    assert all(window // r == ATTN_WINDOW for window, r in ATTN_BRANCHES)
    x = x.reshape(t, d)
    ks = vs = brev = bfar = None
    for layer in range(DEPTH):
        x = ffn_ln(x, ffn_w_in[layer, 0].astype(BF16), ffn_w_out[layer, 0].astype(BF16),
                   ln_g[layer, 0], ln_b[layer, 0])
        if layer < N_SSD_LAYERS:
            z, xbc, dt, dtt = ssd_in(x.reshape(bz, s, d), m_in_proj[layer].astype(BF16),
                                     m_conv_w[layer], m_conv_b[layer], m_dt_bias[layer],
                                     d_inner=d_inner)
            y = ssd_scan(xbc, dt, dtt, m_a_log[layer], m_d[layer], d_inner=d_inner)
            x = ssd_out(x, y.reshape(t, d_inner), z.reshape(t, d_inner), m_norm_w[layer],
                        m_out_proj[layer].astype(BF16), ln_g[layer, 1], ln_b[layer, 1])
        else:
            a = layer - N_SSD_LAYERS
            qs = proj_cm(x.reshape(bz, s, d), a_w_q[a].astype(BF16), dilations=dilations,
                         width=width, scale=HEAD_DIM ** -0.5 * LOG2_E)
            o = attn_mix(qs, ks, vs, brev, bfar, dilations=dilations)
            x = attn_out(x.reshape(bz, s, d), o, a_w_o[a].astype(BF16), ln_g[layer, 1],
                         ln_b[layer, 1]).reshape(t, d)
        x = ffn_ln(x, ffn_w_in[layer, 1].astype(BF16), ffn_w_out[layer, 1].astype(BF16),
                   ln_g[layer, 2], ln_b[layer, 2])
        if layer == N_SSD_LAYERS - 1:
            kvs = proj_cm(x.reshape(bz, s, d), kv_w.astype(BF16), dilations=dilations * 2,
                          width=width)
            ks, vs = kvs[:nb], kvs[nb:]
            brev, bfar = _bias_rows(rel_bias, heads)
    return x.reshape(bz, s, d)
```

```python
import functools
import math

import jax
import jax.numpy as jnp
from jax import lax
from jax.experimental import pallas as pl
from jax.experimental.pallas import tpu as pltpu

F32 = jnp.float32
BF16 = jnp.bfloat16

DEPTH = 4
N_SSD_LAYERS = DEPTH // 2
ALPHA = (2.0 * DEPTH) ** 0.25
EPS = 1e-5
FFN_RES = 0.5
HEAD_DIM = 64
SSD_GROUPS = 8
SSD_STATE = 128
SSD_CHUNK = 128
SSD_CONV = 4
ATTN_BRANCHES = ((128, 1), (512, 4), (2048, 16))
ATTN_WINDOW = 128
ATTN_TILE = ATTN_WINDOW * max(r for _, r in ATTN_BRANCHES)
NUM_BUCKETS = 32
MAX_DISTANCE = 2048
NEG = -1e30
LOG2_E = math.log2(math.e)

LANES = 128
SUBLANES = 8
VMEM_LIMIT_BYTES = 56 * 1024 * 1024

NT_DIMS = (((1,), (1,)), ((), ()))


def _params(semantics):
    return pltpu.CompilerParams(dimension_semantics=semantics,
                                vmem_limit_bytes=VMEM_LIMIT_BYTES)


def _resident(shape):
    zeros = (0,) * len(shape)
    return pl.BlockSpec(shape, lambda *_: zeros, pipeline_mode=pl.Buffered(1))


def _layer_norm(y, g, b):
    mu = jnp.mean(y, axis=-1, keepdims=True)
    d = y - mu
    var = jnp.mean(d * d, axis=-1, keepdims=True)
    return d * lax.rsqrt(var + EPS) * g + b


def _silu(v):
    return v * jax.nn.sigmoid(v)


def _gated_norm_project(y_ref, z_ref, nw_ref, w_ref, rows):
    groups, _, gw = y_ref.shape
    acc = None
    for g in range(groups):
        sl = slice(g * gw, (g + 1) * gw)
        gated = y_ref[g, rows, :].astype(F32) * _silu(z_ref[rows, sl].astype(F32))
        ms = jnp.mean(gated * gated, axis=-1, keepdims=True)
        normed = (gated * lax.rsqrt(ms + EPS) * nw_ref[:, sl]).astype(BF16)
        part = jnp.dot(normed, w_ref[sl, :], preferred_element_type=F32)
        acc = part if acc is None else acc + part
    return acc


def _ffn_kernel(*refs, d_ff, ck, sub, mixer):
    if mixer == "ssd":
        x_ref, y_ref, z_ref, nw_ref, wm_ref, gm_ref, bm_ref = refs[:7]
    elif mixer == "attn":
        x_ref, a_ref, wm_ref, gm_ref, bm_ref = refs[:5]
    else:
        x_ref = refs[0]
    win_ref, wout_ref, g_ref, b_ref, o_ref = refs[-5:]
    for h in range(x_ref.shape[0] // sub):
        rows = slice(h * sub, (h + 1) * sub)
        x = x_ref[rows, :]
        if mixer == "ssd":
            mixed = _gated_norm_project(y_ref, z_ref, nw_ref, wm_ref, rows)
            x = _layer_norm(ALPHA * x + mixed, gm_ref[...], bm_ref[...])
        elif mixer == "attn":
            heads = jnp.concatenate([a_ref[lg, rows, :] for lg in range(a_ref.shape[0])], axis=-1)
            mixed = jnp.dot(heads, wm_ref[...], preferred_element_type=F32)
            x = _layer_norm(ALPHA * x + mixed, gm_ref[...], bm_ref[...])
        xb = x.astype(BF16)
        acc = jnp.zeros(x.shape, F32)
        for c in range(d_ff // ck):
            gate = jnp.dot(xb, win_ref[:, c * ck:(c + 1) * ck], preferred_element_type=F32)
            up = jnp.dot(xb, win_ref[:, d_ff + c * ck:d_ff + (c + 1) * ck],
                         preferred_element_type=F32)
            act = (_silu(gate) * up).astype(BF16)
            acc = acc + jnp.dot(act, wout_ref[c * ck:(c + 1) * ck, :],
                                preferred_element_type=F32)
        o_ref[rows, :] = _layer_norm(ALPHA * x + FFN_RES * acc, g_ref[...], b_ref[...])


def ffn_ln(x, w_in, w_out, g, b, *, ssd=None, attn=None, ck=256, sub=512):
    t, d = x.shape
    d_ff = w_out.shape[0]
    row = lambda v: v.reshape(1, -1)
    if ssd is not None:
        y, z, norm_w, w_proj, gm, bm = ssd
        tm, mixer = sub, "ssd"
        groups, per_batch, gw = y.shape[1], y.shape[2] // tm, y.shape[3]
        pre_specs = [pl.BlockSpec((None, groups, tm, gw),
                                  lambda i: (i // per_batch, 0, i % per_batch, 0)),
                     pl.BlockSpec((tm, z.shape[1]), lambda i: (i, 0)), _resident((1, z.shape[1])),
                     _resident(w_proj.shape), _resident((1, d)), _resident((1, d))]
        pre_args = [y, z, row(norm_w), w_proj, row(gm), row(bm)]
    elif attn is not None:
        heads, w_o, gm, bm = attn
        tm, mixer = 2 * sub, "attn"
        groups, per_batch = heads.shape[1], heads.shape[2] // tm
        pre_specs = [pl.BlockSpec((None, groups, tm, LANES),
                                  lambda i: (i // per_batch, 0, i % per_batch, 0)),
                     _resident(w_o.shape), _resident((1, d)), _resident((1, d))]
        pre_args = [heads, w_o, row(gm), row(bm)]
    else:
        tm, mixer, pre_specs, pre_args = 2 * sub, None, [], []
    return pl.pallas_call(
        functools.partial(_ffn_kernel, d_ff=d_ff, ck=ck, sub=sub, mixer=mixer),
        out_shape=jax.ShapeDtypeStruct((t, d), F32),
        grid=(t // tm,),
        in_specs=[pl.BlockSpec((tm, d), lambda i: (i, 0))] + pre_specs
                 + [_resident(w_in.shape), _resident(w_out.shape),
                    _resident((1, d)), _resident((1, d))],
        out_specs=pl.BlockSpec((tm, d), lambda i: (i, 0)),
        compiler_params=_params(("parallel",)),
        name="ffn_ln" if mixer is None else f"{mixer}_out_ffn_ln",
    )(x, *pre_args, w_in, w_out, row(g), row(b))


def _proj_cm_kernel(x_ref, w_ref, *refs, dilations, width, scale):
    o_refs, scr = refs[:-1], refs[-1]
    xb = x_ref[...].astype(BF16)
    tm = xb.shape[0]
    for n in sorted(range(len(dilations)), key=lambda n: -dilations[n]):
        o_ref, r = o_refs[n], dilations[n]
        y = jnp.dot(xb, w_ref[:, n * width:(n + 1) * width], preferred_element_type=F32) * scale
        if r == 1:
            for lg in range(width // LANES):
                o_ref[lg, 0] = y[:, lg * LANES:(lg + 1) * LANES].astype(BF16)
        else:
            for lg in range(width // LANES):
                scr[lg] = y[:, lg * LANES:(lg + 1) * LANES]
            for lg in range(width // LANES):
                for c in range(r):
                    o_ref[lg, c] = scr[lg, pl.ds(c, tm // r, stride=r), :].astype(BF16)


def proj_cm(x, w, *, dilations, width, scale=1.0, tm=512):
    bz, s, d = x.shape
    groups = width // LANES
    kern = functools.partial(_proj_cm_kernel, dilations=dilations, width=width, scale=scale)
    return pl.pallas_call(
        kern,
        out_shape=tuple(jax.ShapeDtypeStruct((bz, groups, r, s // r, LANES), BF16)
                        for r in dilations),
        grid=(bz, s // tm),
        in_specs=[pl.BlockSpec((None, tm, d), lambda b, i: (b, i, 0)), _resident(w.shape)],
        out_specs=tuple(pl.BlockSpec((None, groups, r, tm // r, LANES),
                                     lambda b, i: (b, 0, 0, i, 0)) for r in dilations),
        scratch_shapes=[pltpu.VMEM((groups, tm, LANES), F32)],
        compiler_params=_params(("parallel", "parallel")),
        name="proj_cm",
    )(x, w)


def _ssd_in_kernel(x_ref, win_ref, wdtt_ref, cw_ref, cb_ref, dtbt_ref,
                   z_ref, xs_ref, bm_ref, cm_ref, dtt_ref, halo_ref, stage_ref,
                   *, tm, cn, d_inner, conv_dim):
    i = pl.program_id(1)
    xs_groups = d_inner // LANES
    per_xs = xs_ref.shape[2] // LANES

    def store_conv(grp, val):
        if grp < xs_groups:
            xs_ref[grp // per_xs, :, (grp % per_xs) * LANES:(grp % per_xs + 1) * LANES] = val
        elif grp < xs_groups + bm_ref.shape[0]:
            bm_ref[grp - xs_groups] = val
        else:
            cm_ref[grp - xs_groups - bm_ref.shape[0]] = val

    xb = x_ref[...].astype(BF16)
    hist = SUBLANES
    per = tm // SUBLANES
    gpc = cn // LANES

    @pl.when(i == 0)
    def _():
        halo_ref[:, 0:hist, :] = jnp.zeros((halo_ref.shape[0], hist, LANES), F32)

    @pl.when(i > 0)
    def _():
        halo_ref[:, 0:hist, :] = halo_ref[:, tm:tm + hist, :]

    for c in range(d_inner // cn):
        sl = slice(c * cn, (c + 1) * cn)
        z_ref[:, sl] = jnp.dot(xb, win_ref[:, sl], preferred_element_type=F32).astype(z_ref.dtype)
    for c in range(conv_dim // cn):
        u = jnp.dot(xb, win_ref[:, d_inner + c * cn:d_inner + (c + 1) * cn],
                    preferred_element_type=F32)
        for lg in range(gpc):
            halo_ref[c * gpc + lg, hist:hist + tm, :] = u[:, lg * LANES:(lg + 1) * LANES]
        for lg in range(gpc):
            grp = c * gpc + lg
            cols = slice(grp * LANES, (grp + 1) * LANES)
            taps = {back: halo_ref[grp, pl.ds(hist - back, per, stride=SUBLANES), :]
                    for back in range(1 - SUBLANES, SSD_CONV)}
            for p in range(SUBLANES):
                conv = cb_ref[:, cols] + jnp.zeros((per, LANES), F32)
                for k in range(SSD_CONV):
                    conv = conv + cw_ref[k:k + 1, cols] * taps[SSD_CONV - 1 - k - p]
                stage_ref[lg, pl.ds(p, per, stride=SUBLANES), :] = _silu(conv)
            store_conv(grp, stage_ref[lg].astype(xs_ref.dtype))
    dtt_raw = lax.dot_general(wdtt_ref[...], xb, NT_DIMS, preferred_element_type=F32)
    dtt_ref[...] = jax.nn.softplus(dtt_raw + dtbt_ref[...])


def ssd_in(x, w_in, conv_w, conv_b, dt_bias, *, d_inner, tm=256, cn=512):
    bz, s, d = x.shape
    conv_dim, heads = conv_w.shape[1], dt_bias.shape[0]
    groups = SSD_GROUPS
    gw, n_state = d_inner // groups, (conv_dim - d_inner) // (2 * groups)
    kern = functools.partial(_ssd_in_kernel, tm=tm, cn=cn, d_inner=d_inner, conv_dim=conv_dim)
    grouped = lambda width: pl.BlockSpec((None, groups, tm, width), lambda b, i: (b, 0, i, 0))
    return pl.pallas_call(
        kern,
        out_shape=(jax.ShapeDtypeStruct((bz, s, d_inner), BF16),
                   jax.ShapeDtypeStruct((bz, groups, s, gw), BF16),
                   jax.ShapeDtypeStruct((bz, groups, s, n_state), BF16),
                   jax.ShapeDtypeStruct((bz, groups, s, n_state), BF16),
                   jax.ShapeDtypeStruct((bz, heads, s), F32)),
        grid=(bz, s // tm),
        in_specs=[pl.BlockSpec((None, tm, d), lambda b, i: (b, i, 0)),
                  _resident(w_in.shape), _resident((heads, d)), _resident(conv_w.shape),
                  _resident((1, conv_dim)), _resident((heads, 1))],
        out_specs=(pl.BlockSpec((None, tm, d_inner), lambda b, i: (b, i, 0)),
                   grouped(gw), grouped(n_state), grouped(n_state),
                   pl.BlockSpec((None, heads, tm), lambda b, i: (b, 0, i))),
        scratch_shapes=[pltpu.VMEM((conv_dim // LANES, tm + SUBLANES, LANES), F32),
                        pltpu.VMEM((cn // LANES, tm, LANES), F32)],
        compiler_params=_params(("parallel", "arbitrary")),
        name="ssd_in",
    )(x, w_in, w_in[:, d_inner + conv_dim:].T, conv_w, conv_b.reshape(1, conv_dim),
      dt_bias.reshape(heads, 1))


def _ssd_scan_kernel(xs_ref, b_ref, c_ref, dtt_ref, alog_ref, alogt_ref, dskip_ref, y_ref,
                     h_ref, *, cps):
    ck = SSD_CHUNK
    heads = dtt_ref.shape[0]
    groups, _, gw = h_ref.shape
    hpg = gw // HEAD_DIM
    per_block = LANES // HEAD_DIM

    @pl.when(pl.program_id(1) == 0)
    def _():
        h_ref[...] = jnp.zeros(h_ref.shape, F32)

    a2_row = -jnp.exp(alog_ref[...]) * LOG2_E
    a2_col = -jnp.exp(alogt_ref[...]) * LOG2_E
    ri = lax.broadcasted_iota(jnp.int32, (ck, ck), 0)
    ci = lax.broadcasted_iota(jnp.int32, (ck, ck), 1)
    causal = ci <= ri
    tril = causal.astype(F32)
    triu = (ri <= ci).astype(F32)
    lane = lax.broadcasted_iota(jnp.int32, (ck, gw), 1)
    lane1 = lax.broadcasted_iota(jnp.int32, (1, gw), 1)
    lane_b = lax.broadcasted_iota(jnp.int32, (ck, LANES), 1)
    head_mask = [(lane >= j * HEAD_DIM) & (lane < (j + 1) * HEAD_DIM) for j in range(hpg)]

    def pick_head(vals, lanes):
        out = vals[-1]
        for j in range(len(vals) - 2, -1, -1):
            out = jnp.where(lanes < (j + 1) * HEAD_DIM, vals[j], out)
        return out

    for q in range(cps):
        rows = slice(q * ck, (q + 1) * ck)
        dtt = dtt_ref[:, rows]
        acum = jnp.dot(tril, dtt.T * a2_row, precision=lax.Precision.HIGHEST,
                       preferred_element_type=F32)
        acumt = jnp.dot(dtt * a2_col, triu, precision=lax.Precision.HIGHEST,
                        preferred_element_type=F32)
        a_end = acumt[:, ck - 1:ck]
        wt = jnp.exp2(a_end - acumt) * dtt
        dec = jnp.broadcast_to(jnp.exp2(a_end), (heads, gw))
        for g in range(groups):
            xs_gb = xs_ref[g, rows, :]
            b_gb = b_ref[g, rows, :]
            c_gb = c_ref[g, rows, :]
            b_gt = b_gb.astype(F32).T
            cb = lax.dot_general(c_gb, b_gb, NT_DIMS, preferred_element_type=F32)
            h_g = h_ref[g]
            y_off = jnp.dot(c_gb, h_g.astype(BF16), preferred_element_type=F32)
            xs_heads = jnp.concatenate(
                [jnp.where(head_mask[j], xs_gb, jnp.zeros_like(xs_gb)) for j in range(hpg)], axis=0)
            m_parts, bw_parts, e_blocks = [], [], []
            for j in range(hpg):
                hd = g * hpg + j
                col = jnp.broadcast_to(acum[:, hd:hd + 1], (ck, ck))
                decay = jnp.exp2(jnp.where(causal, col - acumt[hd:hd + 1, :], -jnp.inf))
                m_parts.append((cb * decay * dtt[hd:hd + 1, :]).astype(BF16))
                bw_parts.append((b_gt * wt[hd:hd + 1, :]).astype(BF16))
                e_col = jnp.exp2(col)
                if j % per_block == 0:
                    e_blocks.append(e_col)
                else:
                    e_blocks[-1] = jnp.where(lane_b >= (j % per_block) * HEAD_DIM, e_col,
                                             e_blocks[-1])
            e_start = jnp.concatenate(e_blocks, axis=1)
            y_d = jnp.dot(jnp.concatenate(m_parts, axis=1), xs_heads,
                          preferred_element_type=F32)
            st = jnp.dot(jnp.concatenate(bw_parts, axis=1), xs_heads,
                         preferred_element_type=F32)
            y_g = y_d + y_off * e_start + dskip_ref[g] * xs_gb.astype(F32)
            y_ref[g, rows, :] = y_g.astype(y_ref.dtype)
            dec_g = dec[g * hpg:(g + 1) * hpg, :]
            h_ref[g] = pick_head([dec_g[j:j + 1, :] for j in range(hpg)], lane1) * h_g + st


def ssd_scan(xs, bm, cm, dtt, a_log, d_skip, *, cps=2):
    bz, groups, s, gw = xs.shape
    n_state = bm.shape[3]
    heads = a_log.shape[0]
    lt = cps * SSD_CHUNK
    seq = lambda width: pl.BlockSpec((None, groups, lt, width), lambda b, i: (b, 0, i, 0))
    return pl.pallas_call(
        functools.partial(_ssd_scan_kernel, cps=cps),
        out_shape=jax.ShapeDtypeStruct((bz, groups, s, gw), BF16),
        grid=(bz, s // lt),
        in_specs=[seq(gw), seq(n_state), seq(n_state),
                  pl.BlockSpec((None, heads, lt), lambda b, i: (b, 0, i)),
                  _resident((1, heads)), _resident((heads, 1)), _resident((groups, 1, gw))],
        out_specs=seq(gw),
        scratch_shapes=[pltpu.VMEM((groups, n_state, gw), F32)],
        compiler_params=_params(("parallel", "arbitrary")),
        name="ssd_scan",
    )(xs, bm, cm, dtt, a_log.reshape(1, heads), a_log.reshape(heads, 1),
      jnp.repeat(d_skip, HEAD_DIM).reshape(groups, 1, gw))


def _attn_kernel(*refs, dilations):
    nb = len(dilations)
    n = ATTN_WINDOW
    q_refs = refs[0:nb]
    kc_refs, kp_refs = refs[nb:3 * nb:2], refs[nb + 1:3 * nb:2]
    vc_refs, vp_refs = refs[3 * nb:5 * nb:2], refs[3 * nb + 1:5 * nb:2]
    brev_ref, bfar_ref, o_ref = refs[5 * nb:5 * nb + 3]
    scratch = refs[5 * nb + 3:]
    kw_refs, vw_refs = scratch[0:nb], scratch[nb:2 * nb]
    bias_scr, o_scr, l_scr, lg_scr, mx_scr = scratch[2 * nb:]
    first_tile = pl.program_id(2) == 0

    ri = lax.broadcasted_iota(jnp.int32, (n, n), 0)
    ki = lax.broadcasted_iota(jnp.int32, (n, n), 1)
    low_q = lax.broadcasted_iota(jnp.int32, (n, LANES), 1) < HEAD_DIM
    low_v = lax.broadcasted_iota(jnp.int32, (2 * n, LANES), 1) < HEAD_DIM
    in_prev = lax.broadcasted_iota(jnp.int32, (1, 2 * n), 1) < n

    for g in range(nb):
        for e in range(2):
            circ = pltpu.roll(jnp.broadcast_to(brev_ref[g, e:e + 1, :], (n, n)), 0, 1,
                              stride=1, stride_axis=0)
            far = bfar_ref[g, e:e + 1, :]
            bias_scr[g, e, :, 0:n] = jnp.where(ki > ri, circ, jnp.where(ki == ri, far, NEG))
            bias_scr[g, e, :, n:2 * n] = jnp.where(ki <= ri, circ, NEG)

    for g in range(nb):
        kw_refs[g][:, 0:n, :] = kp_refs[g][...]
        kw_refs[g][:, n:, :] = kc_refs[g][...]
        vw_refs[g][:, 0:n, :] = vp_refs[g][...]
        vw_refs[g][:, n:, :] = vc_refs[g][...]

    one = jnp.ones((), BF16)
    zero = jnp.zeros((), BF16)
    units = ATTN_TILE // n

    def split(g, idx):
        blocks = units // dilations[g]
        c, j = idx // blocks, idx % blocks
        return c, j, pl.multiple_of(j * n, n)

    def logits_stage(g, idx):
        c, j, start = split(g, idx)
        q_u = q_refs[g][c, pl.ds(start, n), :]
        k_w = kw_refs[g][c, pl.ds(start, 2 * n), :]
        no_prev = jnp.logical_and(first_tile, j == 0)
        neg_row = jnp.where(in_prev, jnp.where(no_prev, NEG, 0.0), 0.0)
        for e in range(2):
            own_q = low_q if e == 0 else ~low_q
            logits = lax.dot_general(jnp.where(own_q, q_u, zero), k_w, NT_DIMS,
                                     preferred_element_type=F32)
            logits = logits + bias_scr[g, e] + neg_row
            lg_scr[g % 2, idx, e] = logits
            mx_scr[g % 2, idx, e] = jnp.broadcast_to(jnp.max(logits, axis=-1, keepdims=True),
                                                     (n, LANES))

    def value_stage(g, idx):
        r = dilations[g]
        c, j, start = split(g, idx)
        v_w = vw_refs[g][c, pl.ds(start, 2 * n), :]
        res = []
        for e in range(2):
            m = mx_scr[g % 2, idx, e]
            p = jnp.concatenate(
                [jnp.exp2(lg_scr[g % 2, idx, e, :, half * n:(half + 1) * n] - m)
                 for half in range(2)], axis=-1).astype(BF16)
            res.append(jnp.dot(p, jnp.where(low_v if e == 0 else ~low_v, v_w, one),
                               preferred_element_type=F32))
        num = jnp.where(low_q, res[0], res[1])
        den = pltpu.roll(jnp.where(low_q, res[1], res[0]), HEAD_DIM, 1)
        rows = pl.ds(j * (n * r) + c, n, stride=r) if r > 1 else pl.ds(start, n)
        o_scr[g, rows, :] = num / den
        l_scr[g, rows, :] = (jnp.where(low_q, mx_scr[g % 2, idx, 0], mx_scr[g % 2, idx, 1])
                             + jnp.log2(den))

    for t in range(nb + 1):
        def body(idx, carry, t=t):
            if t > 0:
                value_stage(t - 1, idx)
            if t < nb:
                logits_stage(t, idx)
            return carry

        lax.fori_loop(0, units, body, 0, unroll=8)

    rb = 2 * n

    def mix(tb, carry):
        rows = pl.ds(pl.multiple_of(tb * rb, rb), rb)
        ls = [l_scr[g, rows, :] for g in range(nb)]
        top = functools.reduce(jnp.maximum, ls)
        ws = [jnp.exp2(l - top) for l in ls]
        acc = functools.reduce(lambda a, b: a + b, [w * o_scr[g, rows, :] for g, w in enumerate(ws)])
        o_ref[rows, :] = (acc / functools.reduce(lambda a, b: a + b, ws)).astype(o_ref.dtype)
        return carry

    lax.fori_loop(0, ATTN_TILE // rb, mix, 0)


def attn_mix(qs, ks, vs, brev, bfar, *, dilations):
    bz, groups, _, s, _ = qs[0].shape
    n, tile = ATTN_WINDOW, ATTN_TILE
    nb = len(dilations)

    def cur(r):
        return pl.BlockSpec((None, None, r, tile // r, LANES), lambda b, hg, i: (b, hg, 0, i, 0))

    def prev(r):
        per_tile = tile // r // n
        return pl.BlockSpec((None, None, r, n, LANES),
                            lambda b, hg, i: (b, hg, 0, jnp.maximum(i * per_tile - 1, 0), 0))

    in_specs = [cur(r) for r in dilations]
    args = list(qs)
    for arrs in (ks, vs):
        for a, r in zip(arrs, dilations):
            in_specs += [cur(r), prev(r)]
            args += [a, a]
    in_specs += [pl.BlockSpec((nb, None, 2, LANES), lambda b, hg, i: (0, hg, 0, 0)),
                 pl.BlockSpec((nb, None, 2, 1), lambda b, hg, i: (0, hg, 0, 0))]
    args += [brev, bfar]
    windows = [pltpu.VMEM((r, n + tile // r, LANES), BF16) for r in dilations]
    return pl.pallas_call(
        functools.partial(_attn_kernel, dilations=dilations),
        out_shape=jax.ShapeDtypeStruct((bz, groups, s, LANES), BF16),
        grid=(bz, groups, s // tile),
        in_specs=in_specs,
        out_specs=pl.BlockSpec((None, None, tile, LANES), lambda b, hg, i: (b, hg, i, 0)),
        scratch_shapes=windows + windows + [pltpu.VMEM((nb, 2, n, 2 * n), F32),
                                            pltpu.VMEM((nb, tile, LANES), F32),
                                            pltpu.VMEM((nb, tile, LANES), F32),
                                            pltpu.VMEM((2, tile // n, 2, n, 2 * n), F32),
                                            pltpu.VMEM((2, tile // n, 2, n, LANES), F32)],
        compiler_params=_params(("parallel", "parallel", "arbitrary")),
        name="attn_mix",
    )(*args)


def _t5_causal_bucket(dist):
    max_exact = NUM_BUCKETS // 2
    logv = (jnp.log(jnp.maximum(dist, 1).astype(F32) / max_exact)
            / math.log(MAX_DISTANCE / max_exact))
    large = jnp.minimum(max_exact + (logv * (NUM_BUCKETS - max_exact)).astype(jnp.int32),
                        NUM_BUCKETS - 1)
    return jnp.where(dist < max_exact, dist, large)


def _bias_rows(rel_bias, heads):
    n = ATTN_WINDOW
    rev, far = [], []
    for branch, (window, r) in enumerate(ATTN_BRANCHES):
        buckets = _t5_causal_bucket(jnp.arange(n + 1, dtype=jnp.int32) * r)
        per_dist = rel_bias[buckets][:, branch * heads:(branch + 1) * heads].T.astype(F32) * LOG2_E
        rev.append(per_dist[:, (-jnp.arange(n)) % n])
        far.append(per_dist[:, n:n + 1])
    return (jnp.stack(rev).reshape(len(rev), heads // 2, 2, n),
            jnp.stack(far).reshape(len(far), heads // 2, 2, 1))


def kernel(x, ln_g, ln_b, ffn_w_in, ffn_w_out, m_in_proj, m_conv_w, m_conv_b, m_dt_bias, m_a_log,
           m_d, m_norm_w, m_out_proj, a_w_q, a_w_o, kv_w, rel_bias):
    bz, s, d = x.shape
    t = bz * s
    d_inner = m_out_proj.shape[1]
    conv_dim = m_conv_w.shape[2]
    width = a_w_o.shape[1]
    heads = width // HEAD_DIM
    dilations = tuple(r for _, r in ATTN_BRANCHES)
    nb = len(dilations)
    assert all(window // r == ATTN_WINDOW for window, r in ATTN_BRANCHES)
    x = x.reshape(t, d)
    ks = vs = brev = bfar = None
    for layer in range(DEPTH):
        x = ffn_ln(x, ffn_w_in[layer, 0].astype(BF16), ffn_w_out[layer, 0].astype(BF16),
                   ln_g[layer, 0], ln_b[layer, 0])
        if layer < N_SSD_LAYERS:
            z, xs, bm, cm, dtt = ssd_in(x.reshape(bz, s, d), m_in_proj[layer].astype(BF16),
                                        m_conv_w[layer], m_conv_b[layer], m_dt_bias[layer],
                                        d_inner=d_inner)
            y = ssd_scan(xs, bm, cm, dtt, m_a_log[layer], m_d[layer])
            mixer = dict(ssd=(y, z.reshape(t, d_inner), m_norm_w[layer],
                              m_out_proj[layer].astype(BF16), ln_g[layer, 1], ln_b[layer, 1]))
        else:
            a = layer - N_SSD_LAYERS
            qs = proj_cm(x.reshape(bz, s, d), a_w_q[a].astype(BF16), dilations=dilations,
                         width=width, scale=HEAD_DIM ** -0.5 * LOG2_E)
            o = attn_mix(qs, ks, vs, brev, bfar, dilations=dilations)
            mixer = dict(attn=(o, a_w_o[a].astype(BF16), ln_g[layer, 1], ln_b[layer, 1]))
        x = ffn_ln(x, ffn_w_in[layer, 1].astype(BF16), ffn_w_out[layer, 1].astype(BF16),
                   ln_g[layer, 2], ln_b[layer, 2], **mixer)
        if layer == N_SSD_LAYERS - 1:
            kvs = proj_cm(x.reshape(bz, s, d), kv_w.astype(BF16), dilations=dilations * 2,
                          width=width)
            ks, vs = kvs[:nb], kvs[nb:]
            brev, bfar = _bias_rows(rel_bias, heads)
    return x.reshape(bz, s, d)
```

```python
import functools
import math

import jax
import jax.numpy as jnp
from jax import lax
from jax.experimental import pallas as pl
from jax.experimental.pallas import tpu as pltpu

F32 = jnp.float32
BF16 = jnp.bfloat16

DEPTH = 4
N_SSD_LAYERS = DEPTH // 2
ALPHA = (2.0 * DEPTH) ** 0.25
EPS = 1e-5
FFN_RES = 0.5
HEAD_DIM = 64
SSD_GROUPS = 8
SSD_STATE = 128
SSD_CHUNK = 128
SSD_CONV = 4
ATTN_BRANCHES = ((128, 1), (512, 4), (2048, 16))
ATTN_WINDOW = 128
ATTN_TILE = ATTN_WINDOW * max(r for _, r in ATTN_BRANCHES)
NUM_BUCKETS = 32
MAX_DISTANCE = 2048
NEG = -1e30
LOG2_E = math.log2(math.e)

LANES = 128
SUBLANES = 8
VMEM_LIMIT_BYTES = 56 * 1024 * 1024

NT_DIMS = (((1,), (1,)), ((), ()))


def _params(semantics):
    return pltpu.CompilerParams(dimension_semantics=semantics,
                                vmem_limit_bytes=VMEM_LIMIT_BYTES)


def _resident(shape, lead=()):
    block = (None,) * len(lead) + tuple(shape[len(lead):])
    index = tuple(lead) + (0,) * (len(shape) - len(lead))
    return pl.BlockSpec(block, lambda *_: index, pipeline_mode=pl.Buffered(1))


def _layer_norm(y, g, b):
    mu = jnp.mean(y, axis=-1, keepdims=True)
    d = y - mu
    var = jnp.mean(d * d, axis=-1, keepdims=True)
    return d * lax.rsqrt(var + EPS) * g + b


def _silu(v):
    return v * jax.nn.sigmoid(v)


def _gated_norm_project(y_ref, z_ref, nw_ref, w_ref, rows):
    groups, _, gw = y_ref.shape
    acc = None
    for g in range(groups):
        sl = slice(g * gw, (g + 1) * gw)
        gated = y_ref[g, rows, :].astype(F32) * _silu(z_ref[rows, sl].astype(F32))
        ms = jnp.mean(gated * gated, axis=-1, keepdims=True)
        normed = (gated * lax.rsqrt(ms + EPS) * nw_ref[:, sl]).astype(BF16)
        part = jnp.dot(normed, w_ref[sl, :], preferred_element_type=F32)
        acc = part if acc is None else acc + part
    return acc


def _ffn_kernel(*refs, d_ff, ck, sub, mixer):
    if mixer == "ssd":
        x_ref, y_ref, z_ref, nw_ref, wm_ref, gm_ref, bm_ref = refs[:7]
    elif mixer == "attn":
        x_ref, a_ref, wm_ref, gm_ref, bm_ref = refs[:5]
    else:
        x_ref = refs[0]
    win_ref, wout_ref, g_ref, b_ref, o_ref = refs[-5:]
    for h in range(x_ref.shape[0] // sub):
        rows = slice(h * sub, (h + 1) * sub)
        x = x_ref[rows, :]
        if mixer == "ssd":
            mixed = _gated_norm_project(y_ref, z_ref, nw_ref, wm_ref, rows)
            x = _layer_norm(ALPHA * x + mixed, gm_ref[...], bm_ref[...])
        elif mixer == "attn":
            heads = jnp.concatenate([a_ref[lg, rows, :] for lg in range(a_ref.shape[0])], axis=-1)
            mixed = jnp.dot(heads, wm_ref[...], preferred_element_type=F32)
            x = _layer_norm(ALPHA * x + mixed, gm_ref[...], bm_ref[...])
        xb = x.astype(BF16)
        acc = jnp.zeros(x.shape, F32)
        for c in range(d_ff // ck):
            gate = jnp.dot(xb, win_ref[:, c * ck:(c + 1) * ck], preferred_element_type=F32)
            up = jnp.dot(xb, win_ref[:, d_ff + c * ck:d_ff + (c + 1) * ck],
                         preferred_element_type=F32)
            act = (_silu(gate) * up).astype(BF16)
            acc = acc + jnp.dot(act, wout_ref[c * ck:(c + 1) * ck, :],
                                preferred_element_type=F32)
        o_ref[rows, :] = _layer_norm(ALPHA * x + FFN_RES * acc, g_ref[...], b_ref[...])


def ffn_ln(x, w_in, w_out, g, b, *, lead=(), ssd=None, attn=None, ck=256, sub=512):
    t, d = x.shape
    d_ff = w_out.shape[-2]
    row = lambda v: v.reshape(1, -1)
    if ssd is not None:
        y, z, norm_w, w_proj, proj_lead, gm, bm = ssd
        tm, mixer = sub, "ssd"
        groups, per_batch, gw = y.shape[1], y.shape[2] // tm, y.shape[3]
        pre_specs = [pl.BlockSpec((None, groups, tm, gw),
                                  lambda i: (i // per_batch, 0, i % per_batch, 0)),
                     pl.BlockSpec((tm, z.shape[1]), lambda i: (i, 0)), _resident((1, z.shape[1])),
                     _resident(w_proj.shape, proj_lead), _resident((1, d)), _resident((1, d))]
        pre_args = [y, z, row(norm_w), w_proj, row(gm), row(bm)]
    elif attn is not None:
        heads, w_o, o_lead, gm, bm = attn
        tm, mixer = 2 * sub, "attn"
        groups, per_batch = heads.shape[1], heads.shape[2] // tm
        pre_specs = [pl.BlockSpec((None, groups, tm, LANES),
                                  lambda i: (i // per_batch, 0, i % per_batch, 0)),
                     _resident(w_o.shape, o_lead), _resident((1, d)), _resident((1, d))]
        pre_args = [heads, w_o, row(gm), row(bm)]
    else:
        tm, mixer, pre_specs, pre_args = 2 * sub, None, [], []
    return pl.pallas_call(
        functools.partial(_ffn_kernel, d_ff=d_ff, ck=ck, sub=sub, mixer=mixer),
        out_shape=jax.ShapeDtypeStruct((t, d), F32),
        grid=(t // tm,),
        in_specs=[pl.BlockSpec((tm, d), lambda i: (i, 0))] + pre_specs
                 + [_resident(w_in.shape, lead), _resident(w_out.shape, lead),
                    _resident((1, d)), _resident((1, d))],
        out_specs=pl.BlockSpec((tm, d), lambda i: (i, 0)),
        compiler_params=_params(("parallel",)),
        name="ffn_ln" if mixer is None else f"{mixer}_out_ffn_ln",
    )(x, *pre_args, w_in, w_out, row(g), row(b))


def _proj_cm_kernel(x_ref, w_ref, *refs, dilations, width, scale, sub):
    o_refs, scr = refs[:-1], refs[-1]
    for h in range(x_ref.shape[0] // sub):
        xb = x_ref[h * sub:(h + 1) * sub, :].astype(BF16)
        for n in sorted(range(len(dilations)), key=lambda n: -dilations[n]):
            o_ref, r = o_refs[n], dilations[n]
            out_rows = slice(h * sub // r, (h + 1) * sub // r)
            y = jnp.dot(xb, w_ref[:, n * width:(n + 1) * width],
                        preferred_element_type=F32) * scale
            if r == 1:
                for lg in range(width // LANES):
                    o_ref[lg, 0, out_rows, :] = y[:, lg * LANES:(lg + 1) * LANES].astype(BF16)
            else:
                for lg in range(width // LANES):
                    scr[h, lg] = y[:, lg * LANES:(lg + 1) * LANES]
                for lg in range(width // LANES):
                    for c in range(r):
                        o_ref[lg, c, out_rows, :] = (
                            scr[h, lg, pl.ds(c, sub // r, stride=r), :].astype(BF16))


def proj_cm(x, w, *, dilations, width, lead=(), scale=1.0, tm=1024, sub=512):
    bz, s, d = x.shape
    groups = width // LANES
    kern = functools.partial(_proj_cm_kernel, dilations=dilations, width=width, scale=scale,
                             sub=sub)
    return pl.pallas_call(
        kern,
        out_shape=tuple(jax.ShapeDtypeStruct((bz, groups, r, s // r, LANES), BF16)
                        for r in dilations),
        grid=(bz, s // tm),
        in_specs=[pl.BlockSpec((None, tm, d), lambda b, i: (b, i, 0)), _resident(w.shape, lead)],
        out_specs=tuple(pl.BlockSpec((None, groups, r, tm // r, LANES),
                                     lambda b, i: (b, 0, 0, i, 0)) for r in dilations),
        scratch_shapes=[pltpu.VMEM((tm // sub, groups, sub, LANES), F32)],
        compiler_params=_params(("parallel", "parallel")),
        name="proj_cm",
    )(x, w)


def _ssd_in_kernel(x_ref, win_ref, wdtt_ref, cw_ref, cb_ref, dtbt_ref,
                   z_ref, xs_ref, bm_ref, cm_ref, dtt_ref, halo_ref, stage_ref,
                   *, tm, cn, d_inner, conv_dim):
    i = pl.program_id(1)
    xs_groups = d_inner // LANES
    per_xs = xs_ref.shape[2] // LANES

    def store_conv(grp, val):
        if grp < xs_groups:
            xs_ref[grp // per_xs, :, (grp % per_xs) * LANES:(grp % per_xs + 1) * LANES] = val
        elif grp < xs_groups + bm_ref.shape[0]:
            bm_ref[grp - xs_groups] = val
        else:
            cm_ref[grp - xs_groups - bm_ref.shape[0]] = val

    xb = x_ref[...].astype(BF16)
    hist = SUBLANES
    per = tm // SUBLANES
    gpc = cn // LANES

    @pl.when(i == 0)
    def _():
        halo_ref[:, 0:hist, :] = jnp.zeros((halo_ref.shape[0], hist, LANES), F32)

    @pl.when(i > 0)
    def _():
        halo_ref[:, 0:hist, :] = halo_ref[:, tm:tm + hist, :]

    for c in range(d_inner // cn):
        sl = slice(c * cn, (c + 1) * cn)
        z_ref[:, sl] = jnp.dot(xb, win_ref[:, sl], preferred_element_type=F32).astype(z_ref.dtype)
    for c in range(conv_dim // cn):
        u = jnp.dot(xb, win_ref[:, d_inner + c * cn:d_inner + (c + 1) * cn],
                    preferred_element_type=F32)
        for lg in range(gpc):
            halo_ref[c * gpc + lg, hist:hist + tm, :] = u[:, lg * LANES:(lg + 1) * LANES]
        for lg in range(gpc):
            grp = c * gpc + lg
            cols = slice(grp * LANES, (grp + 1) * LANES)
            taps = {back: halo_ref[grp, pl.ds(hist - back, per, stride=SUBLANES), :]
                    for back in range(1 - SUBLANES, SSD_CONV)}
            for p in range(SUBLANES):
                conv = cb_ref[:, cols] + jnp.zeros((per, LANES), F32)
                for k in range(SSD_CONV):
                    conv = conv + cw_ref[k:k + 1, cols] * taps[SSD_CONV - 1 - k - p]
                stage_ref[lg, pl.ds(p, per, stride=SUBLANES), :] = _silu(conv)
            store_conv(grp, stage_ref[lg].astype(xs_ref.dtype))
    dtt_raw = lax.dot_general(wdtt_ref[...], xb, NT_DIMS, preferred_element_type=F32)
    dtt_ref[...] = jax.nn.softplus(dtt_raw + dtbt_ref[...])


def ssd_in(x, w_in, conv_w, conv_b, dt_bias, *, d_inner, lead=(), tm=256, cn=512):
    bz, s, d = x.shape
    conv_dim, heads = conv_w.shape[1], dt_bias.shape[0]
    groups = SSD_GROUPS
    gw, n_state = d_inner // groups, (conv_dim - d_inner) // (2 * groups)
    kern = functools.partial(_ssd_in_kernel, tm=tm, cn=cn, d_inner=d_inner, conv_dim=conv_dim)
    grouped = lambda width: pl.BlockSpec((None, groups, tm, width), lambda b, i: (b, 0, i, 0))
    return pl.pallas_call(
        kern,
        out_shape=(jax.ShapeDtypeStruct((bz, s, d_inner), BF16),
                   jax.ShapeDtypeStruct((bz, groups, s, gw), BF16),
                   jax.ShapeDtypeStruct((bz, groups, s, n_state), BF16),
                   jax.ShapeDtypeStruct((bz, groups, s, n_state), BF16),
                   jax.ShapeDtypeStruct((bz, heads, s), F32)),
        grid=(bz, s // tm),
        in_specs=[pl.BlockSpec((None, tm, d), lambda b, i: (b, i, 0)),
                  _resident(w_in.shape, lead), _resident((heads, d)), _resident(conv_w.shape),
                  _resident((1, conv_dim)), _resident((heads, 1))],
        out_specs=(pl.BlockSpec((None, tm, d_inner), lambda b, i: (b, i, 0)),
                   grouped(gw), grouped(n_state), grouped(n_state),
                   pl.BlockSpec((None, heads, tm), lambda b, i: (b, 0, i))),
        scratch_shapes=[pltpu.VMEM((conv_dim // LANES, tm + SUBLANES, LANES), F32),
                        pltpu.VMEM((cn // LANES, tm, LANES), F32)],
        compiler_params=_params(("parallel", "arbitrary")),
        name="ssd_in",
    )(x, w_in, w_in[lead][:, d_inner + conv_dim:].T, conv_w, conv_b.reshape(1, conv_dim),
      dt_bias.reshape(heads, 1))


def _ssd_scan_kernel(xs_ref, b_ref, c_ref, dtt_ref, alog_ref, alogt_ref, dskip_ref, y_ref,
                     h_ref, *, cps):
    ck = SSD_CHUNK
    heads = dtt_ref.shape[0]
    groups, _, gw = h_ref.shape
    hpg = gw // HEAD_DIM
    per_block = LANES // HEAD_DIM

    @pl.when(pl.program_id(1) == 0)
    def _():
        h_ref[...] = jnp.zeros(h_ref.shape, F32)

    a2_row = -jnp.exp(alog_ref[...]) * LOG2_E
    a2_col = -jnp.exp(alogt_ref[...]) * LOG2_E
    ri = lax.broadcasted_iota(jnp.int32, (ck, ck), 0)
    ci = lax.broadcasted_iota(jnp.int32, (ck, ck), 1)
    causal = ci <= ri
    tril = causal.astype(F32)
    triu = (ri <= ci).astype(F32)
    lane = lax.broadcasted_iota(jnp.int32, (ck, gw), 1)
    lane1 = lax.broadcasted_iota(jnp.int32, (1, gw), 1)
    lane_b = lax.broadcasted_iota(jnp.int32, (ck, LANES), 1)
    head_mask = [(lane >= j * HEAD_DIM) & (lane < (j + 1) * HEAD_DIM) for j in range(hpg)]

    def pick_head(vals, lanes):
        out = vals[-1]
        for j in range(len(vals) - 2, -1, -1):
            out = jnp.where(lanes < (j + 1) * HEAD_DIM, vals[j], out)
        return out

    for q in range(cps):
        rows = slice(q * ck, (q + 1) * ck)
        dtt = dtt_ref[:, rows]
        acum = jnp.dot(tril, dtt.T * a2_row, precision=lax.Precision.HIGHEST,
                       preferred_element_type=F32)
        acumt = jnp.dot(dtt * a2_col, triu, precision=lax.Precision.HIGHEST,
                        preferred_element_type=F32)
        a_end = acumt[:, ck - 1:ck]
        wt = jnp.exp2(a_end - acumt) * dtt
        dec = jnp.broadcast_to(jnp.exp2(a_end), (heads, gw))
        for g in range(groups):
            xs_gb = xs_ref[g, rows, :]
            b_gb = b_ref[g, rows, :]
            c_gb = c_ref[g, rows, :]
            b_gt = b_gb.astype(F32).T
            cb = lax.dot_general(c_gb, b_gb, NT_DIMS, preferred_element_type=F32)
            h_g = h_ref[g]
            y_off = jnp.dot(c_gb, h_g.astype(BF16), preferred_element_type=F32)
            xs_heads = jnp.concatenate(
                [jnp.where(head_mask[j], xs_gb, jnp.zeros_like(xs_gb)) for j in range(hpg)], axis=0)
            m_parts, bw_parts, e_blocks = [], [], []
            for j in range(hpg):
                hd = g * hpg + j
                col = jnp.broadcast_to(acum[:, hd:hd + 1], (ck, ck))
                decay = jnp.exp2(jnp.where(causal, col - acumt[hd:hd + 1, :], -jnp.inf))
                m_parts.append((cb * decay * dtt[hd:hd + 1, :]).astype(BF16))
                bw_parts.append((b_gt * wt[hd:hd + 1, :]).astype(BF16))
                e_col = jnp.exp2(col)
                if j % per_block == 0:
                    e_blocks.append(e_col)
                else:
                    e_blocks[-1] = jnp.where(lane_b >= (j % per_block) * HEAD_DIM, e_col,
                                             e_blocks[-1])
            e_start = jnp.concatenate(e_blocks, axis=1)
            y_d = jnp.dot(jnp.concatenate(m_parts, axis=1), xs_heads,
                          preferred_element_type=F32)
            st = jnp.dot(jnp.concatenate(bw_parts, axis=1), xs_heads,
                         preferred_element_type=F32)
            y_g = y_d + y_off * e_start + dskip_ref[g] * xs_gb.astype(F32)
            y_ref[g, rows, :] = y_g.astype(y_ref.dtype)
            dec_g = dec[g * hpg:(g + 1) * hpg, :]
            h_ref[g] = pick_head([dec_g[j:j + 1, :] for j in range(hpg)], lane1) * h_g + st


def ssd_scan(xs, bm, cm, dtt, a_log, d_skip, *, cps=2):
    bz, groups, s, gw = xs.shape
    n_state = bm.shape[3]
    heads = a_log.shape[0]
    lt = cps * SSD_CHUNK
    seq = lambda width: pl.BlockSpec((None, groups, lt, width), lambda b, i: (b, 0, i, 0))
    return pl.pallas_call(
        functools.partial(_ssd_scan_kernel, cps=cps),
        out_shape=jax.ShapeDtypeStruct((bz, groups, s, gw), BF16),
        grid=(bz, s // lt),
        in_specs=[seq(gw), seq(n_state), seq(n_state),
                  pl.BlockSpec((None, heads, lt), lambda b, i: (b, 0, i)),
                  _resident((1, heads)), _resident((heads, 1)), _resident((groups, 1, gw))],
        out_specs=seq(gw),
        scratch_shapes=[pltpu.VMEM((groups, n_state, gw), F32)],
        compiler_params=_params(("parallel", "arbitrary")),
        name="ssd_scan",
    )(xs, bm, cm, dtt, a_log.reshape(1, heads), a_log.reshape(heads, 1),
      jnp.repeat(d_skip, HEAD_DIM).reshape(groups, 1, gw))


def _attn_kernel(*refs, dilations):
    nb = len(dilations)
    n = ATTN_WINDOW
    q_refs = refs[0:nb]
    kc_refs, kp_refs = refs[nb:3 * nb:2], refs[nb + 1:3 * nb:2]
    vc_refs, vp_refs = refs[3 * nb:5 * nb:2], refs[3 * nb + 1:5 * nb:2]
    brev_ref, bfar_ref, o_ref = refs[5 * nb:5 * nb + 3]
    scratch = refs[5 * nb + 3:]
    kw_refs, vw_refs = scratch[0:nb], scratch[nb:2 * nb]
    bias_scr, o_scr, l_scr, lg_scr, mx_scr = scratch[2 * nb:]
    first_tile = pl.program_id(2) == 0

    ri = lax.broadcasted_iota(jnp.int32, (n, n), 0)
    ki = lax.broadcasted_iota(jnp.int32, (n, n), 1)
    low_q = lax.broadcasted_iota(jnp.int32, (n, LANES), 1) < HEAD_DIM
    low_v = lax.broadcasted_iota(jnp.int32, (2 * n, LANES), 1) < HEAD_DIM

    for g in range(nb):
        for e in range(2):
            circ = pltpu.roll(jnp.broadcast_to(brev_ref[g, e:e + 1, :], (n, n)), 0, 1,
                              stride=1, stride_axis=0)
            far = bfar_ref[g, e:e + 1, :]
            current = jnp.where(ki <= ri, circ, NEG)
            bias_scr[g, e, 0, :, 0:n] = jnp.where(ki > ri, circ, jnp.where(ki == ri, far, NEG))
            bias_scr[g, e, 0, :, n:2 * n] = current
            bias_scr[g, e, 1, :, 0:n] = jnp.full((n, n), NEG, F32)
            bias_scr[g, e, 1, :, n:2 * n] = current

    for g in range(nb):
        kw_refs[g][:, 0:n, :] = kp_refs[g][...]
        kw_refs[g][:, n:, :] = kc_refs[g][...]
        vw_refs[g][:, 0:n, :] = vp_refs[g][...]
        vw_refs[g][:, n:, :] = vc_refs[g][...]

    one = jnp.ones((), BF16)
    zero = jnp.zeros((), BF16)
    units = ATTN_TILE // n

    def split(g, idx):
        blocks = units // dilations[g]
        c, j = idx // blocks, idx % blocks
        return c, j, pl.multiple_of(j * n, n)

    def logits_stage(g, idx):
        c, j, start = split(g, idx)
        q_u = q_refs[g][c, pl.ds(start, n), :]
        k_w = kw_refs[g][c, pl.ds(start, 2 * n), :]
        variant = jnp.logical_and(first_tile, j == 0).astype(jnp.int32)
        for e in range(2):
            own_q = low_q if e == 0 else ~low_q
            logits = lax.dot_general(jnp.where(own_q, q_u, zero), k_w, NT_DIMS,
                                     preferred_element_type=F32)
            logits = logits + bias_scr[g, e, variant]
            lg_scr[g % 2, idx, e] = logits
            mx_scr[g % 2, idx, e] = jnp.broadcast_to(jnp.max(logits, axis=-1, keepdims=True),
                                                     (n, LANES))

    def value_stage(g, idx):
        r = dilations[g]
        c, j, start = split(g, idx)
        v_w = vw_refs[g][c, pl.ds(start, 2 * n), :]
        res = []
        for e in range(2):
            m = mx_scr[g % 2, idx, e]
            p = jnp.concatenate(
                [jnp.exp2(lg_scr[g % 2, idx, e, :, half * n:(half + 1) * n] - m)
                 for half in range(2)], axis=-1).astype(BF16)
            res.append(jnp.dot(p, jnp.where(low_v if e == 0 else ~low_v, v_w, one),
                               preferred_element_type=F32))
        num = jnp.where(low_q, res[0], res[1])
        den = pltpu.roll(jnp.where(low_q, res[1], res[0]), HEAD_DIM, 1)
        rows = pl.ds(j * (n * r) + c, n, stride=r) if r > 1 else pl.ds(start, n)
        o_scr[g, rows, :] = num / den
        l_scr[g, rows, :] = (jnp.where(low_q, mx_scr[g % 2, idx, 0], mx_scr[g % 2, idx, 1])
                             + jnp.log2(den))

    for t in range(nb + 1):
        def body(idx, carry, t=t):
            if t > 0:
                value_stage(t - 1, idx)
            if t < nb:
                logits_stage(t, idx)
            return carry

        lax.fori_loop(0, units, body, 0, unroll=8)

    rb = 2 * n

    def mix(tb, carry):
        rows = pl.ds(pl.multiple_of(tb * rb, rb), rb)
        ls = [l_scr[g, rows, :] for g in range(nb)]
        top = functools.reduce(jnp.maximum, ls)
        ws = [jnp.exp2(l - top) for l in ls]
        acc = functools.reduce(lambda a, b: a + b, [w * o_scr[g, rows, :] for g, w in enumerate(ws)])
        o_ref[rows, :] = (acc / functools.reduce(lambda a, b: a + b, ws)).astype(o_ref.dtype)
        return carry

    lax.fori_loop(0, ATTN_TILE // rb, mix, 0)


def attn_mix(qs, ks, vs, brev, bfar, *, dilations):
    bz, groups, _, s, _ = qs[0].shape
    n, tile = ATTN_WINDOW, ATTN_TILE
    nb = len(dilations)

    def cur(r):
        return pl.BlockSpec((None, None, r, tile // r, LANES), lambda b, hg, i: (b, hg, 0, i, 0))

    def prev(r):
        per_tile = tile // r // n
        return pl.BlockSpec((None, None, r, n, LANES),
                            lambda b, hg, i: (b, hg, 0, jnp.maximum(i * per_tile - 1, 0), 0))

    in_specs = [cur(r) for r in dilations]
    args = list(qs)
    for arrs in (ks, vs):
        for a, r in zip(arrs, dilations):
            in_specs += [cur(r), prev(r)]
            args += [a, a]
    in_specs += [pl.BlockSpec((nb, None, 2, LANES), lambda b, hg, i: (0, hg, 0, 0)),
                 pl.BlockSpec((nb, None, 2, 1), lambda b, hg, i: (0, hg, 0, 0))]
    args += [brev, bfar]
    windows = [pltpu.VMEM((r, n + tile // r, LANES), BF16) for r in dilations]
    return pl.pallas_call(
        functools.partial(_attn_kernel, dilations=dilations),
        out_shape=jax.ShapeDtypeStruct((bz, groups, s, LANES), BF16),
        grid=(bz, groups, s // tile),
        in_specs=in_specs,
        out_specs=pl.BlockSpec((None, None, tile, LANES), lambda b, hg, i: (b, hg, i, 0)),
        scratch_shapes=windows + windows + [pltpu.VMEM((nb, 2, 2, n, 2 * n), F32),
                                            pltpu.VMEM((nb, tile, LANES), F32),
                                            pltpu.VMEM((nb, tile, LANES), F32),
                                            pltpu.VMEM((2, tile // n, 2, n, 2 * n), F32),
                                            pltpu.VMEM((2, tile // n, 2, n, LANES), F32)],
        compiler_params=_params(("parallel", "parallel", "arbitrary")),
        name="attn_mix",
    )(*args)


def _t5_causal_bucket(dist):
    max_exact = NUM_BUCKETS // 2
    logv = (jnp.log(jnp.maximum(dist, 1).astype(F32) / max_exact)
            / math.log(MAX_DISTANCE / max_exact))
    large = jnp.minimum(max_exact + (logv * (NUM_BUCKETS - max_exact)).astype(jnp.int32),
                        NUM_BUCKETS - 1)
    return jnp.where(dist < max_exact, dist, large)


def _bias_rows(rel_bias, heads):
    n = ATTN_WINDOW
    rev, far = [], []
    for branch, (window, r) in enumerate(ATTN_BRANCHES):
        buckets = _t5_causal_bucket(jnp.arange(n + 1, dtype=jnp.int32) * r)
        per_dist = rel_bias[buckets][:, branch * heads:(branch + 1) * heads].T.astype(F32) * LOG2_E
        rev.append(per_dist[:, (-jnp.arange(n)) % n])
        far.append(per_dist[:, n:n + 1])
    return (jnp.stack(rev).reshape(len(rev), heads // 2, 2, n),
            jnp.stack(far).reshape(len(far), heads // 2, 2, 1))


def kernel(x, ln_g, ln_b, ffn_w_in, ffn_w_out, m_in_proj, m_conv_w, m_conv_b, m_dt_bias, m_a_log,
           m_d, m_norm_w, m_out_proj, a_w_q, a_w_o, kv_w, rel_bias):
    bz, s, d = x.shape
    t = bz * s
    d_inner = m_out_proj.shape[1]
    conv_dim = m_conv_w.shape[2]
    width = a_w_o.shape[1]
    heads = width // HEAD_DIM
    dilations = tuple(r for _, r in ATTN_BRANCHES)
    nb = len(dilations)
    assert all(window // r == ATTN_WINDOW for window, r in ATTN_BRANCHES)
    x = x.reshape(t, d)
    ffn_w_in, ffn_w_out, m_in_proj, m_out_proj, a_w_q, a_w_o, kv_w = (
        w.astype(BF16) for w in (ffn_w_in, ffn_w_out, m_in_proj, m_out_proj, a_w_q, a_w_o, kv_w))
    ks = vs = brev = bfar = None
    for layer in range(DEPTH):
        x = ffn_ln(x, ffn_w_in, ffn_w_out, ln_g[layer, 0], ln_b[layer, 0], lead=(layer, 0))
        if layer < N_SSD_LAYERS:
            z, xs, bm, cm, dtt = ssd_in(x.reshape(bz, s, d), m_in_proj, m_conv_w[layer],
                                        m_conv_b[layer], m_dt_bias[layer], d_inner=d_inner,
                                        lead=(layer,))
            y = ssd_scan(xs, bm, cm, dtt, m_a_log[layer], m_d[layer])
            mixer = dict(ssd=(y, z.reshape(t, d_inner), m_norm_w[layer], m_out_proj, (layer,),
                              ln_g[layer, 1], ln_b[layer, 1]))
        else:
            a = layer - N_SSD_LAYERS
            qs = proj_cm(x.reshape(bz, s, d), a_w_q, dilations=dilations, width=width,
                         lead=(a,), scale=HEAD_DIM ** -0.5 * LOG2_E)
            o = attn_mix(qs, ks, vs, brev, bfar, dilations=dilations)
            mixer = dict(attn=(o, a_w_o, (a,), ln_g[layer, 1], ln_b[layer, 1]))
        x = ffn_ln(x, ffn_w_in, ffn_w_out, ln_g[layer, 2], ln_b[layer, 2], lead=(layer, 1),
                   **mixer)
        if layer == N_SSD_LAYERS - 1:
            kvs = proj_cm(x.reshape(bz, s, d), kv_w, dilations=dilations * 2, width=width)
            ks, vs = kvs[:nb], kvs[nb:]
            brev, bfar = _bias_rows(rel_bias, heads)
    return x.reshape(bz, s, d)
```

```python
import functools
import math

import jax
import jax.numpy as jnp
from jax import lax
from jax.experimental import pallas as pl
from jax.experimental.pallas import tpu as pltpu

F32 = jnp.float32
BF16 = jnp.bfloat16

DEPTH = 4
N_SSD_LAYERS = DEPTH // 2
ALPHA = (2.0 * DEPTH) ** 0.25
EPS = 1e-5
FFN_RES = 0.5
HEAD_DIM = 64
SSD_GROUPS = 8
SSD_STATE = 128
SSD_CHUNK = 128
SSD_CONV = 4
ATTN_BRANCHES = ((128, 1), (512, 4), (2048, 16))
ATTN_WINDOW = 128
ATTN_TILE = ATTN_WINDOW * max(r for _, r in ATTN_BRANCHES)
NUM_BUCKETS = 32
MAX_DISTANCE = 2048
NEG = -1e30
LOG2_E = math.log2(math.e)

LANES = 128
SUBLANES = 8
VMEM_LIMIT_BYTES = 56 * 1024 * 1024

NT_DIMS = (((1,), (1,)), ((), ()))


def _params(semantics):
    return pltpu.CompilerParams(dimension_semantics=semantics,
                                vmem_limit_bytes=VMEM_LIMIT_BYTES)


def _resident(shape, lead=()):
    block = (None,) * len(lead) + tuple(shape[len(lead):])
    index = tuple(lead) + (0,) * (len(shape) - len(lead))
    return pl.BlockSpec(block, lambda *_: index, pipeline_mode=pl.Buffered(1))


def _layer_norm(y, g, b):
    mu = jnp.mean(y, axis=-1, keepdims=True)
    d = y - mu
    var = jnp.mean(d * d, axis=-1, keepdims=True)
    return d * lax.rsqrt(var + EPS) * g + b


def _silu(v):
    return v * jax.nn.sigmoid(v)


def _gated_norm_project(y_ref, z_ref, nw_ref, w_ref, rows):
    groups, _, gw = y_ref.shape
    acc = None
    for g in range(groups):
        sl = slice(g * gw, (g + 1) * gw)
        gated = y_ref[g, rows, :].astype(F32) * _silu(z_ref[rows, sl].astype(F32))
        ms = jnp.mean(gated * gated, axis=-1, keepdims=True)
        normed = (gated * lax.rsqrt(ms + EPS) * nw_ref[:, sl]).astype(BF16)
        part = jnp.dot(normed, w_ref[sl, :], preferred_element_type=F32)
        acc = part if acc is None else acc + part
    return acc


def _ffn_kernel(*refs, d_ff, ck, sub, mixer):
    if mixer == "ssd":
        x_ref, y_ref, z_ref, nw_ref, wm_ref, gm_ref, bm_ref = refs[:7]
    elif mixer == "attn":
        x_ref, a_ref, wm_ref, gm_ref, bm_ref = refs[:5]
    else:
        x_ref = refs[0]
    win_ref, wout_ref, g_ref, b_ref, o_ref = refs[-5:]
    for h in range(x_ref.shape[0] // sub):
        rows = slice(h * sub, (h + 1) * sub)
        x = x_ref[rows, :]
        if mixer == "ssd":
            mixed = _gated_norm_project(y_ref, z_ref, nw_ref, wm_ref, rows)
            x = _layer_norm(ALPHA * x + mixed, gm_ref[...], bm_ref[...])
        elif mixer == "attn":
            heads = jnp.concatenate([a_ref[lg, rows, :] for lg in range(a_ref.shape[0])], axis=-1)
            mixed = jnp.dot(heads, wm_ref[...], preferred_element_type=F32)
            x = _layer_norm(ALPHA * x + mixed, gm_ref[...], bm_ref[...])
        xb = x.astype(BF16)
        acc = jnp.zeros(x.shape, F32)
        for c in range(d_ff // ck):
            gate = jnp.dot(xb, win_ref[:, c * ck:(c + 1) * ck], preferred_element_type=F32)
            up = jnp.dot(xb, win_ref[:, d_ff + c * ck:d_ff + (c + 1) * ck],
                         preferred_element_type=F32)
            act = (_silu(gate) * up).astype(BF16)
            acc = acc + jnp.dot(act, wout_ref[c * ck:(c + 1) * ck, :],
                                preferred_element_type=F32)
        o_ref[rows, :] = _layer_norm(ALPHA * x + FFN_RES * acc, g_ref[...], b_ref[...])


def ffn_ln(x, w_in, w_out, g, b, *, lead=(), ssd=None, attn=None, ck=256, sub=512):
    t, d = x.shape
    d_ff = w_out.shape[-2]
    row = lambda v: v.reshape(1, -1)
    if ssd is not None:
        y, z, norm_w, w_proj, proj_lead, gm, bm = ssd
        tm, mixer = sub, "ssd"
        groups, per_batch, gw = y.shape[1], y.shape[2] // tm, y.shape[3]
        pre_specs = [pl.BlockSpec((None, groups, tm, gw),
                                  lambda i: (i // per_batch, 0, i % per_batch, 0)),
                     pl.BlockSpec((tm, z.shape[1]), lambda i: (i, 0)), _resident((1, z.shape[1])),
                     _resident(w_proj.shape, proj_lead), _resident((1, d)), _resident((1, d))]
        pre_args = [y, z, row(norm_w), w_proj, row(gm), row(bm)]
    elif attn is not None:
        heads, w_o, o_lead, gm, bm = attn
        tm, mixer = 2 * sub, "attn"
        groups, per_batch = heads.shape[1], heads.shape[2] // tm
        pre_specs = [pl.BlockSpec((None, groups, tm, LANES),
                                  lambda i: (i // per_batch, 0, i % per_batch, 0)),
                     _resident(w_o.shape, o_lead), _resident((1, d)), _resident((1, d))]
        pre_args = [heads, w_o, row(gm), row(bm)]
    else:
        tm, mixer, pre_specs, pre_args = 2 * sub, None, [], []
    return pl.pallas_call(
        functools.partial(_ffn_kernel, d_ff=d_ff, ck=ck, sub=sub, mixer=mixer),
        out_shape=jax.ShapeDtypeStruct((t, d), F32),
        grid=(t // tm,),
        in_specs=[pl.BlockSpec((tm, d), lambda i: (i, 0))] + pre_specs
                 + [_resident(w_in.shape, lead), _resident(w_out.shape, lead),
                    _resident((1, d)), _resident((1, d))],
        out_specs=pl.BlockSpec((tm, d), lambda i: (i, 0)),
        compiler_params=_params(("parallel",)),
        name="ffn_ln" if mixer is None else f"{mixer}_out_ffn_ln",
    )(x, *pre_args, w_in, w_out, row(g), row(b))


def _proj_cm_kernel(x_ref, w_ref, *refs, dilations, width, scale, sub):
    o_refs, scr = refs[:-1], refs[-1]
    for h in range(x_ref.shape[0] // sub):
        xb = x_ref[h * sub:(h + 1) * sub, :].astype(BF16)
        for n in sorted(range(len(dilations)), key=lambda n: -dilations[n]):
            o_ref, r = o_refs[n], dilations[n]
            out_rows = slice(h * sub // r, (h + 1) * sub // r)
            y = jnp.dot(xb, w_ref[:, n * width:(n + 1) * width],
                        preferred_element_type=F32) * scale
            if r == 1:
                for lg in range(width // LANES):
                    o_ref[lg, 0, out_rows, :] = y[:, lg * LANES:(lg + 1) * LANES].astype(BF16)
            else:
                for lg in range(width // LANES):
                    scr[h, lg] = y[:, lg * LANES:(lg + 1) * LANES]
                for lg in range(width // LANES):
                    for c in range(r):
                        o_ref[lg, c, out_rows, :] = (
                            scr[h, lg, pl.ds(c, sub // r, stride=r), :].astype(BF16))


def proj_cm(x, w, *, dilations, width, lead=(), scale=1.0, tm=1024, sub=512):
    bz, s, d = x.shape
    groups = width // LANES
    kern = functools.partial(_proj_cm_kernel, dilations=dilations, width=width, scale=scale,
                             sub=sub)
    return pl.pallas_call(
        kern,
        out_shape=tuple(jax.ShapeDtypeStruct((bz, groups, r, s // r, LANES), BF16)
                        for r in dilations),
        grid=(bz, s // tm),
        in_specs=[pl.BlockSpec((None, tm, d), lambda b, i: (b, i, 0)), _resident(w.shape, lead)],
        out_specs=tuple(pl.BlockSpec((None, groups, r, tm // r, LANES),
                                     lambda b, i: (b, 0, 0, i, 0)) for r in dilations),
        scratch_shapes=[pltpu.VMEM((tm // sub, groups, sub, LANES), F32)],
        compiler_params=_params(("parallel", "parallel")),
        name="proj_cm",
    )(x, w)


def _ssd_in_kernel(x_ref, win_ref, wdtt_ref, cw_ref, cb_ref, dtbt_ref,
                   z_ref, xs_ref, bm_ref, cm_ref, dtt_ref, halo_ref, stage_ref,
                   *, tm, cn, d_inner, conv_dim):
    i = pl.program_id(1)
    xs_groups = d_inner // LANES
    per_xs = xs_ref.shape[2] // LANES

    def store_conv(grp, val):
        if grp < xs_groups:
            xs_ref[grp // per_xs, :, (grp % per_xs) * LANES:(grp % per_xs + 1) * LANES] = val
        elif grp < xs_groups + bm_ref.shape[0]:
            bm_ref[grp - xs_groups] = val
        else:
            cm_ref[grp - xs_groups - bm_ref.shape[0]] = val

    xb = x_ref[...].astype(BF16)
    hist = SUBLANES
    per = tm // SUBLANES
    gpc = cn // LANES

    @pl.when(i == 0)
    def _():
        halo_ref[:, 0:hist, :] = jnp.zeros((halo_ref.shape[0], hist, LANES), F32)

    @pl.when(i > 0)
    def _():
        halo_ref[:, 0:hist, :] = halo_ref[:, tm:tm + hist, :]

    for c in range(d_inner // cn):
        sl = slice(c * cn, (c + 1) * cn)
        z_ref[:, sl] = jnp.dot(xb, win_ref[:, sl], preferred_element_type=F32).astype(z_ref.dtype)
    for c in range(conv_dim // cn):
        u = jnp.dot(xb, win_ref[:, d_inner + c * cn:d_inner + (c + 1) * cn],
                    preferred_element_type=F32)
        for lg in range(gpc):
            halo_ref[c * gpc + lg, hist:hist + tm, :] = u[:, lg * LANES:(lg + 1) * LANES]
        for lg in range(gpc):
            grp = c * gpc + lg
            cols = slice(grp * LANES, (grp + 1) * LANES)
            taps = {back: halo_ref[grp, pl.ds(hist - back, per, stride=SUBLANES), :]
                    for back in range(1 - SUBLANES, SSD_CONV)}
            for p in range(SUBLANES):
                conv = cb_ref[:, cols] + jnp.zeros((per, LANES), F32)
                for k in range(SSD_CONV):
                    conv = conv + cw_ref[k:k + 1, cols] * taps[SSD_CONV - 1 - k - p]
                stage_ref[lg, pl.ds(p, per, stride=SUBLANES), :] = _silu(conv)
            store_conv(grp, stage_ref[lg].astype(xs_ref.dtype))
    dtt_raw = lax.dot_general(wdtt_ref[...], xb, NT_DIMS, preferred_element_type=F32)
    dtt_ref[...] = jax.nn.softplus(dtt_raw + dtbt_ref[...])


def ssd_in(x, w_in, conv_w, conv_b, dt_bias, *, d_inner, lead=(), tm=256, cn=512):
    bz, s, d = x.shape
    conv_dim, heads = conv_w.shape[1], dt_bias.shape[0]
    groups = SSD_GROUPS
    gw, n_state = d_inner // groups, (conv_dim - d_inner) // (2 * groups)
    kern = functools.partial(_ssd_in_kernel, tm=tm, cn=cn, d_inner=d_inner, conv_dim=conv_dim)
    grouped = lambda width: pl.BlockSpec((None, groups, tm, width), lambda b, i: (b, 0, i, 0))
    return pl.pallas_call(
        kern,
        out_shape=(jax.ShapeDtypeStruct((bz, s, d_inner), BF16),
                   jax.ShapeDtypeStruct((bz, groups, s, gw), BF16),
                   jax.ShapeDtypeStruct((bz, groups, s, n_state), BF16),
                   jax.ShapeDtypeStruct((bz, groups, s, n_state), BF16),
                   jax.ShapeDtypeStruct((bz, heads, s), F32)),
        grid=(bz, s // tm),
        in_specs=[pl.BlockSpec((None, tm, d), lambda b, i: (b, i, 0)),
                  _resident(w_in.shape, lead), _resident((heads, d)), _resident(conv_w.shape),
                  _resident((1, conv_dim)), _resident((heads, 1))],
        out_specs=(pl.BlockSpec((None, tm, d_inner), lambda b, i: (b, i, 0)),
                   grouped(gw), grouped(n_state), grouped(n_state),
                   pl.BlockSpec((None, heads, tm), lambda b, i: (b, 0, i))),
        scratch_shapes=[pltpu.VMEM((conv_dim // LANES, tm + SUBLANES, LANES), F32),
                        pltpu.VMEM((cn // LANES, tm, LANES), F32)],
        compiler_params=_params(("parallel", "arbitrary")),
        name="ssd_in",
    )(x, w_in, w_in[lead][:, d_inner + conv_dim:].T, conv_w, conv_b.reshape(1, conv_dim),
      dt_bias.reshape(heads, 1))


def _ssd_scan_kernel(xs_ref, b_ref, c_ref, dtt_ref, alog_ref, alogt_ref, dskip_ref, y_ref,
                     h_ref, *, cps):
    ck = SSD_CHUNK
    heads = dtt_ref.shape[0]
    groups, _, gw = h_ref.shape
    hpg = gw // HEAD_DIM
    per_block = LANES // HEAD_DIM

    @pl.when(pl.program_id(1) == 0)
    def _():
        h_ref[...] = jnp.zeros(h_ref.shape, F32)

    a2_row = -jnp.exp(alog_ref[...]) * LOG2_E
    a2_col = -jnp.exp(alogt_ref[...]) * LOG2_E
    ri = lax.broadcasted_iota(jnp.int32, (ck, ck), 0)
    ci = lax.broadcasted_iota(jnp.int32, (ck, ck), 1)
    causal = ci <= ri
    tril = causal.astype(F32)
    triu = (ri <= ci).astype(F32)
    lane = lax.broadcasted_iota(jnp.int32, (ck, gw), 1)
    lane1 = lax.broadcasted_iota(jnp.int32, (1, gw), 1)
    lane_b = lax.broadcasted_iota(jnp.int32, (ck, LANES), 1)
    head_mask = [(lane >= j * HEAD_DIM) & (lane < (j + 1) * HEAD_DIM) for j in range(hpg)]

    def pick_head(vals, lanes):
        out = vals[-1]
        for j in range(len(vals) - 2, -1, -1):
            out = jnp.where(lanes < (j + 1) * HEAD_DIM, vals[j], out)
        return out

    for q in range(cps):
        rows = slice(q * ck, (q + 1) * ck)
        dtt = dtt_ref[:, rows]
        acum = jnp.dot(tril, dtt.T * a2_row, precision=lax.Precision.HIGHEST,
                       preferred_element_type=F32)
        acumt = jnp.dot(dtt * a2_col, triu, precision=lax.Precision.HIGHEST,
                        preferred_element_type=F32)
        a_end = acumt[:, ck - 1:ck]
        wt = jnp.exp2(a_end - acumt) * dtt
        dec = jnp.broadcast_to(jnp.exp2(a_end), (heads, gw))
        for g in range(groups):
            xs_gb = xs_ref[g, rows, :]
            b_gb = b_ref[g, rows, :]
            c_gb = c_ref[g, rows, :]
            b_gt = b_gb.astype(F32).T
            cb = lax.dot_general(c_gb, b_gb, NT_DIMS, preferred_element_type=F32)
            h_g = h_ref[g]
            y_off = jnp.dot(c_gb, h_g.astype(BF16), preferred_element_type=F32)
            xs_heads = jnp.concatenate(
                [jnp.where(head_mask[j], xs_gb, jnp.zeros_like(xs_gb)) for j in range(hpg)], axis=0)
            m_parts, bw_parts, e_blocks = [], [], []
            for j in range(hpg):
                hd = g * hpg + j
                col = jnp.broadcast_to(acum[:, hd:hd + 1], (ck, ck))
                decay = jnp.exp2(jnp.where(causal, col - acumt[hd:hd + 1, :], -jnp.inf))
                m_parts.append((cb * decay * dtt[hd:hd + 1, :]).astype(BF16))
                bw_parts.append((b_gt * wt[hd:hd + 1, :]).astype(BF16))
                e_col = jnp.exp2(col)
                if j % per_block == 0:
                    e_blocks.append(e_col)
                else:
                    e_blocks[-1] = jnp.where(lane_b >= (j % per_block) * HEAD_DIM, e_col,
                                             e_blocks[-1])
            e_start = jnp.concatenate(e_blocks, axis=1)
            y_d = jnp.dot(jnp.concatenate(m_parts, axis=1), xs_heads,
                          preferred_element_type=F32)
            st = jnp.dot(jnp.concatenate(bw_parts, axis=1), xs_heads,
                         preferred_element_type=F32)
            y_g = y_d + y_off * e_start + dskip_ref[g] * xs_gb.astype(F32)
            y_ref[g, rows, :] = y_g.astype(y_ref.dtype)
            dec_g = dec[g * hpg:(g + 1) * hpg, :]
            h_ref[g] = pick_head([dec_g[j:j + 1, :] for j in range(hpg)], lane1) * h_g + st


def ssd_scan(xs, bm, cm, dtt, a_log, d_skip, *, cps=4):
    bz, groups, s, gw = xs.shape
    n_state = bm.shape[3]
    heads = a_log.shape[0]
    lt = cps * SSD_CHUNK
    seq = lambda width: pl.BlockSpec((None, groups, lt, width), lambda b, i: (b, 0, i, 0))
    return pl.pallas_call(
        functools.partial(_ssd_scan_kernel, cps=cps),
        out_shape=jax.ShapeDtypeStruct((bz, groups, s, gw), BF16),
        grid=(bz, s // lt),
        in_specs=[seq(gw), seq(n_state), seq(n_state),
                  pl.BlockSpec((None, heads, lt), lambda b, i: (b, 0, i)),
                  _resident((1, heads)), _resident((heads, 1)), _resident((groups, 1, gw))],
        out_specs=seq(gw),
        scratch_shapes=[pltpu.VMEM((groups, n_state, gw), F32)],
        compiler_params=_params(("parallel", "arbitrary")),
        name="ssd_scan",
    )(xs, bm, cm, dtt, a_log.reshape(1, heads), a_log.reshape(heads, 1),
      jnp.repeat(d_skip, HEAD_DIM).reshape(groups, 1, gw))


def _attn_kernel(*refs, dilations):
    nb = len(dilations)
    n = ATTN_WINDOW
    q_refs = refs[0:nb]
    kc_refs, kp_refs = refs[nb:3 * nb:2], refs[nb + 1:3 * nb:2]
    vc_refs, vp_refs = refs[3 * nb:5 * nb:2], refs[3 * nb + 1:5 * nb:2]
    brev_ref, bfar_ref, o_ref = refs[5 * nb:5 * nb + 3]
    scratch = refs[5 * nb + 3:]
    kw_refs, vw_refs = scratch[0:nb], scratch[nb:2 * nb]
    bias_scr, o_scr, l_scr, lg_scr, mx_scr = scratch[2 * nb:]
    first_tile = pl.program_id(2) == 0

    ri = lax.broadcasted_iota(jnp.int32, (n, n), 0)
    ki = lax.broadcasted_iota(jnp.int32, (n, n), 1)
    low_q = lax.broadcasted_iota(jnp.int32, (n, LANES), 1) < HEAD_DIM
    low_v = lax.broadcasted_iota(jnp.int32, (2 * n, LANES), 1) < HEAD_DIM

    for g in range(nb):
        for e in range(2):
            circ = pltpu.roll(jnp.broadcast_to(brev_ref[g, e:e + 1, :], (n, n)), 0, 1,
                              stride=1, stride_axis=0)
            far = bfar_ref[g, e:e + 1, :]
            current = jnp.where(ki <= ri, circ, NEG)
            bias_scr[g, e, 0, :, 0:n] = jnp.where(ki > ri, circ, jnp.where(ki == ri, far, NEG))
            bias_scr[g, e, 0, :, n:2 * n] = current
            bias_scr[g, e, 1, :, 0:n] = jnp.full((n, n), NEG, F32)
            bias_scr[g, e, 1, :, n:2 * n] = current

    for g in range(nb):
        kw_refs[g][:, 0:n, :] = kp_refs[g][...]
        kw_refs[g][:, n:, :] = kc_refs[g][...]
        vw_refs[g][:, 0:n, :] = vp_refs[g][...]
        vw_refs[g][:, n:, :] = vc_refs[g][...]

    one = jnp.ones((), BF16)
    zero = jnp.zeros((), BF16)
    units = ATTN_TILE // n

    def split(g, idx):
        blocks = units // dilations[g]
        c, j = idx // blocks, idx % blocks
        return c, j, pl.multiple_of(j * n, n)

    def logits_stage(g, idx):
        c, j, start = split(g, idx)
        q_u = q_refs[g][c, pl.ds(start, n), :]
        k_w = kw_refs[g][c, pl.ds(start, 2 * n), :]
        variant = jnp.logical_and(first_tile, j == 0).astype(jnp.int32)
        for e in range(2):
            own_q = low_q if e == 0 else ~low_q
            logits = lax.dot_general(jnp.where(own_q, q_u, zero), k_w, NT_DIMS,
                                     preferred_element_type=F32)
            logits = logits + bias_scr[g, e, variant]
            lg_scr[g % 2, idx, e] = logits
            mx_scr[g % 2, idx, e] = jnp.broadcast_to(jnp.max(logits, axis=-1, keepdims=True),
                                                     (n, LANES))

    def value_stage(g, idx):
        r = dilations[g]
        c, j, start = split(g, idx)
        v_w = vw_refs[g][c, pl.ds(start, 2 * n), :]
        res = []
        for e in range(2):
            m = mx_scr[g % 2, idx, e]
            p = jnp.concatenate(
                [jnp.exp2(lg_scr[g % 2, idx, e, :, half * n:(half + 1) * n] - m)
                 for half in range(2)], axis=-1).astype(BF16)
            res.append(jnp.dot(p, jnp.where(low_v if e == 0 else ~low_v, v_w, one),
                               preferred_element_type=F32))
        num = jnp.where(low_q, res[0], res[1])
        den = pltpu.roll(jnp.where(low_q, res[1], res[0]), HEAD_DIM, 1)
        rows = pl.ds(j * (n * r) + c, n, stride=r) if r > 1 else pl.ds(start, n)
        o_scr[g, rows, :] = num / den
        l_scr[g, rows, :] = (jnp.where(low_q, mx_scr[g % 2, idx, 0], mx_scr[g % 2, idx, 1])
                             + jnp.log2(den))

    for t in range(nb + 1):
        def body(idx, carry, t=t):
            if t > 0:
                value_stage(t - 1, idx)
            if t < nb:
                logits_stage(t, idx)
            return carry

        lax.fori_loop(0, units, body, 0, unroll=16)

    rb = 2 * n

    def mix(tb, carry):
        rows = pl.ds(pl.multiple_of(tb * rb, rb), rb)
        ls = [l_scr[g, rows, :] for g in range(nb)]
        top = functools.reduce(jnp.maximum, ls)
        ws = [jnp.exp2(l - top) for l in ls]
        acc = functools.reduce(lambda a, b: a + b, [w * o_scr[g, rows, :] for g, w in enumerate(ws)])
        o_ref[rows, :] = (acc / functools.reduce(lambda a, b: a + b, ws)).astype(o_ref.dtype)
        return carry

    lax.fori_loop(0, ATTN_TILE // rb, mix, 0)


def attn_mix(qs, ks, vs, brev, bfar, *, dilations):
    bz, groups, _, s, _ = qs[0].shape
    n, tile = ATTN_WINDOW, ATTN_TILE
    nb = len(dilations)

    def cur(r):
        return pl.BlockSpec((None, None, r, tile // r, LANES), lambda b, hg, i: (b, hg, 0, i, 0))

    def prev(r):
        per_tile = tile // r // n
        return pl.BlockSpec((None, None, r, n, LANES),
                            lambda b, hg, i: (b, hg, 0, jnp.maximum(i * per_tile - 1, 0), 0))

    in_specs = [cur(r) for r in dilations]
    args = list(qs)
    for arrs in (ks, vs):
        for a, r in zip(arrs, dilations):
            in_specs += [cur(r), prev(r)]
            args += [a, a]
    in_specs += [pl.BlockSpec((nb, None, 2, LANES), lambda b, hg, i: (0, hg, 0, 0)),
                 pl.BlockSpec((nb, None, 2, 1), lambda b, hg, i: (0, hg, 0, 0))]
    args += [brev, bfar]
    windows = [pltpu.VMEM((r, n + tile // r, LANES), BF16) for r in dilations]
    return pl.pallas_call(
        functools.partial(_attn_kernel, dilations=dilations),
        out_shape=jax.ShapeDtypeStruct((bz, groups, s, LANES), BF16),
        grid=(bz, groups, s // tile),
        in_specs=in_specs,
        out_specs=pl.BlockSpec((None, None, tile, LANES), lambda b, hg, i: (b, hg, i, 0)),
        scratch_shapes=windows + windows + [pltpu.VMEM((nb, 2, 2, n, 2 * n), F32),
                                            pltpu.VMEM((nb, tile, LANES), F32),
                                            pltpu.VMEM((nb, tile, LANES), F32),
                                            pltpu.VMEM((2, tile // n, 2, n, 2 * n), F32),
                                            pltpu.VMEM((2, tile // n, 2, n, LANES), F32)],
        compiler_params=_params(("parallel", "parallel", "arbitrary")),
        name="attn_mix",
    )(*args)


def _t5_causal_bucket(dist):
    max_exact = NUM_BUCKETS // 2
    logv = (jnp.log(jnp.maximum(dist, 1).astype(F32) / max_exact)
            / math.log(MAX_DISTANCE / max_exact))
    large = jnp.minimum(max_exact + (logv * (NUM_BUCKETS - max_exact)).astype(jnp.int32),
                        NUM_BUCKETS - 1)
    return jnp.where(dist < max_exact, dist, large)


def _bias_rows(rel_bias, heads):
    n = ATTN_WINDOW
    rev, far = [], []
    for branch, (window, r) in enumerate(ATTN_BRANCHES):
        buckets = _t5_causal_bucket(jnp.arange(n + 1, dtype=jnp.int32) * r)
        per_dist = rel_bias[buckets][:, branch * heads:(branch + 1) * heads].T.astype(F32) * LOG2_E
        rev.append(per_dist[:, (-jnp.arange(n)) % n])
        far.append(per_dist[:, n:n + 1])
    return (jnp.stack(rev).reshape(len(rev), heads // 2, 2, n),
            jnp.stack(far).reshape(len(far), heads // 2, 2, 1))


def kernel(x, ln_g, ln_b, ffn_w_in, ffn_w_out, m_in_proj, m_conv_w, m_conv_b, m_dt_bias, m_a_log,
           m_d, m_norm_w, m_out_proj, a_w_q, a_w_o, kv_w, rel_bias):
    bz, s, d = x.shape
    t = bz * s
    d_inner = m_out_proj.shape[1]
    conv_dim = m_conv_w.shape[2]
    width = a_w_o.shape[1]
    heads = width // HEAD_DIM
    dilations = tuple(r for _, r in ATTN_BRANCHES)
    nb = len(dilations)
    assert all(window // r == ATTN_WINDOW for window, r in ATTN_BRANCHES)
    x = x.reshape(t, d)
    ffn_w_in, ffn_w_out, m_in_proj, m_out_proj, a_w_q, a_w_o, kv_w = (
        w.astype(BF16) for w in (ffn_w_in, ffn_w_out, m_in_proj, m_out_proj, a_w_q, a_w_o, kv_w))
    ks = vs = brev = bfar = None
    for layer in range(DEPTH):
        x = ffn_ln(x, ffn_w_in, ffn_w_out, ln_g[layer, 0], ln_b[layer, 0], lead=(layer, 0))
        if layer < N_SSD_LAYERS:
            z, xs, bm, cm, dtt = ssd_in(x.reshape(bz, s, d), m_in_proj, m_conv_w[layer],
                                        m_conv_b[layer], m_dt_bias[layer], d_inner=d_inner,
                                        lead=(layer,))
            y = ssd_scan(xs, bm, cm, dtt, m_a_log[layer], m_d[layer])
            mixer = dict(ssd=(y, z.reshape(t, d_inner), m_norm_w[layer], m_out_proj, (layer,),
                              ln_g[layer, 1], ln_b[layer, 1]))
        else:
            a = layer - N_SSD_LAYERS
            qs = proj_cm(x.reshape(bz, s, d), a_w_q, dilations=dilations, width=width,
                         lead=(a,), scale=HEAD_DIM ** -0.5 * LOG2_E)
            o = attn_mix(qs, ks, vs, brev, bfar, dilations=dilations)
            mixer = dict(attn=(o, a_w_o, (a,), ln_g[layer, 1], ln_b[layer, 1]))
        x = ffn_ln(x, ffn_w_in, ffn_w_out, ln_g[layer, 2], ln_b[layer, 2], lead=(layer, 1),
                   **mixer)
        if layer == N_SSD_LAYERS - 1:
            kvs = proj_cm(x.reshape(bz, s, d), kv_w, dilations=dilations * 2, width=width)
            ks, vs = kvs[:nb], kvs[nb:]
            brev, bfar = _bias_rows(rel_bias, heads)
    return x.reshape(bz, s, d)
```

```python
import functools
import math

import jax
import jax.numpy as jnp
from jax import lax
from jax.experimental import pallas as pl
from jax.experimental.pallas import tpu as pltpu

F32 = jnp.float32
BF16 = jnp.bfloat16

DEPTH = 4
N_SSD_LAYERS = DEPTH // 2
ALPHA = (2.0 * DEPTH) ** 0.25
EPS = 1e-5
FFN_RES = 0.5
HEAD_DIM = 64
SSD_GROUPS = 8
SSD_STATE = 128
SSD_CHUNK = 128
SSD_CONV = 4
ATTN_BRANCHES = ((128, 1), (512, 4), (2048, 16))
ATTN_WINDOW = 128
ATTN_TILE = ATTN_WINDOW * max(r for _, r in ATTN_BRANCHES)
NUM_BUCKETS = 32
MAX_DISTANCE = 2048
NEG = -math.inf
LOG2_E = math.log2(math.e)

LANES = 128
SUBLANES = 8
VMEM_LIMIT_BYTES = 56 * 1024 * 1024

NT_DIMS = (((1,), (1,)), ((), ()))


def _params(semantics):
    return pltpu.CompilerParams(dimension_semantics=semantics,
                                vmem_limit_bytes=VMEM_LIMIT_BYTES)


def _resident(shape, lead=()):
    block = (None,) * len(lead) + tuple(shape[len(lead):])
    index = tuple(lead) + (0,) * (len(shape) - len(lead))
    return pl.BlockSpec(block, lambda *_: index, pipeline_mode=pl.Buffered(1))


def _layer_norm(y, g, b):
    mu = jnp.mean(y, axis=-1, keepdims=True)
    d = y - mu
    var = jnp.mean(d * d, axis=-1, keepdims=True)
    return d * lax.rsqrt(var + EPS) * g + b


def _silu(v):
    return v * jax.nn.sigmoid(v)


def _gated_norm_project(y_ref, z_ref, nw_ref, w_ref, rows):
    groups, _, gw = y_ref.shape
    acc = None
    for g in range(groups):
        sl = slice(g * gw, (g + 1) * gw)
        gated = y_ref[g, rows, :].astype(F32) * _silu(z_ref[rows, sl].astype(F32))
        ms = jnp.mean(gated * gated, axis=-1, keepdims=True)
        normed = (gated * lax.rsqrt(ms + EPS) * nw_ref[:, sl]).astype(BF16)
        part = jnp.dot(normed, w_ref[sl, :], preferred_element_type=F32)
        acc = part if acc is None else acc + part
    return acc


def _ffn_kernel(*refs, d_ff, ck, sub, mixer):
    if mixer == "ssd":
        x_ref, y_ref, z_ref, nw_ref, wm_ref, gm_ref, bm_ref = refs[:7]
    elif mixer == "attn":
        x_ref, a_ref, wm_ref, gm_ref, bm_ref = refs[:5]
    else:
        x_ref = refs[0]
    win_ref, wout_ref, g_ref, b_ref, o_ref = refs[-5:]
    for h in range(x_ref.shape[0] // sub):
        rows = slice(h * sub, (h + 1) * sub)
        x = x_ref[rows, :]
        if mixer == "ssd":
            mixed = _gated_norm_project(y_ref, z_ref, nw_ref, wm_ref, rows)
            x = _layer_norm(ALPHA * x + mixed, gm_ref[...], bm_ref[...])
        elif mixer == "attn":
            heads = jnp.concatenate([a_ref[lg, rows, :] for lg in range(a_ref.shape[0])], axis=-1)
            mixed = jnp.dot(heads, wm_ref[...], preferred_element_type=F32)
            x = _layer_norm(ALPHA * x + mixed, gm_ref[...], bm_ref[...])
        xb = x.astype(BF16)
        acc = jnp.zeros(x.shape, F32)
        for c in range(d_ff // ck):
            gate = jnp.dot(xb, win_ref[:, c * ck:(c + 1) * ck], preferred_element_type=F32)
            up = jnp.dot(xb, win_ref[:, d_ff + c * ck:d_ff + (c + 1) * ck],
                         preferred_element_type=F32)
            act = (_silu(gate) * up).astype(BF16)
            acc = acc + jnp.dot(act, wout_ref[c * ck:(c + 1) * ck, :],
                                preferred_element_type=F32)
        o_ref[rows, :] = _layer_norm(ALPHA * x + FFN_RES * acc, g_ref[...], b_ref[...])


def ffn_ln(x, w_in, w_out, g, b, *, lead=(), ssd=None, attn=None, ck=256, sub=512):
    t, d = x.shape
    d_ff = w_out.shape[-2]
    row = lambda v: v.reshape(1, -1)
    if ssd is not None:
        y, z, norm_w, w_proj, proj_lead, gm, bm = ssd
        tm, mixer = sub, "ssd"
        groups, per_batch, gw = y.shape[1], y.shape[2] // tm, y.shape[3]
        pre_specs = [pl.BlockSpec((None, groups, tm, gw),
                                  lambda i: (i // per_batch, 0, i % per_batch, 0)),
                     pl.BlockSpec((tm, z.shape[1]), lambda i: (i, 0)), _resident((1, z.shape[1])),
                     _resident(w_proj.shape, proj_lead), _resident((1, d)), _resident((1, d))]
        pre_args = [y, z, row(norm_w), w_proj, row(gm), row(bm)]
    elif attn is not None:
        heads, w_o, o_lead, gm, bm = attn
        tm, mixer = 2 * sub, "attn"
        groups, per_batch = heads.shape[1], heads.shape[2] // tm
        pre_specs = [pl.BlockSpec((None, groups, tm, LANES),
                                  lambda i: (i // per_batch, 0, i % per_batch, 0)),
                     _resident(w_o.shape, o_lead), _resident((1, d)), _resident((1, d))]
        pre_args = [heads, w_o, row(gm), row(bm)]
    else:
        tm, mixer, pre_specs, pre_args = 2 * sub, None, [], []
    return pl.pallas_call(
        functools.partial(_ffn_kernel, d_ff=d_ff, ck=ck, sub=sub, mixer=mixer),
        out_shape=jax.ShapeDtypeStruct((t, d), F32),
        grid=(t // tm,),
        in_specs=[pl.BlockSpec((tm, d), lambda i: (i, 0))] + pre_specs
                 + [_resident(w_in.shape, lead), _resident(w_out.shape, lead),
                    _resident((1, d)), _resident((1, d))],
        out_specs=pl.BlockSpec((tm, d), lambda i: (i, 0)),
        compiler_params=_params(("parallel",)),
        name="ffn_ln" if mixer is None else f"{mixer}_out_ffn_ln",
    )(x, *pre_args, w_in, w_out, row(g), row(b))


def _proj_cm_kernel(x_ref, w_ref, *refs, dilations, width, scale, sub):
    o_refs, scr = refs[:-1], refs[-1]
    for h in range(x_ref.shape[0] // sub):
        xb = x_ref[h * sub:(h + 1) * sub, :].astype(BF16)
        for n in sorted(range(len(dilations)), key=lambda n: -dilations[n]):
            o_ref, r = o_refs[n], dilations[n]
            out_rows = slice(h * sub // r, (h + 1) * sub // r)
            y = jnp.dot(xb, w_ref[:, n * width:(n + 1) * width],
                        preferred_element_type=F32) * scale
            if r == 1:
                for lg in range(width // LANES):
                    o_ref[lg, 0, out_rows, :] = y[:, lg * LANES:(lg + 1) * LANES].astype(BF16)
            else:
                for lg in range(width // LANES):
                    scr[h, lg] = y[:, lg * LANES:(lg + 1) * LANES]
                for lg in range(width // LANES):
                    for c in range(r):
                        o_ref[lg, c, out_rows, :] = (
                            scr[h, lg, pl.ds(c, sub // r, stride=r), :].astype(BF16))


def proj_cm(x, w, *, dilations, width, lead=(), scale=1.0, tm=1024, sub=1024):
    bz, s, d = x.shape
    groups = width // LANES
    kern = functools.partial(_proj_cm_kernel, dilations=dilations, width=width, scale=scale,
                             sub=sub)
    return pl.pallas_call(
        kern,
        out_shape=tuple(jax.ShapeDtypeStruct((bz, groups, r, s // r, LANES), BF16)
                        for r in dilations),
        grid=(bz, s // tm),
        in_specs=[pl.BlockSpec((None, tm, d), lambda b, i: (b, i, 0)), _resident(w.shape, lead)],
        out_specs=tuple(pl.BlockSpec((None, groups, r, tm // r, LANES),
                                     lambda b, i: (b, 0, 0, i, 0)) for r in dilations),
        scratch_shapes=[pltpu.VMEM((tm // sub, groups, sub, LANES), F32)],
        compiler_params=_params(("parallel", "parallel")),
        name="proj_cm",
    )(x, w)


def _ssd_in_kernel(x_ref, win_ref, wdtt_ref, cw_ref, cb_ref, dtbt_ref,
                   z_ref, xs_ref, bm_ref, cm_ref, dtt_ref, halo_ref, stage_ref,
                   *, tm, cn, d_inner, conv_dim):
    i = pl.program_id(1)
    xs_groups = d_inner // LANES
    per_xs = xs_ref.shape[2] // LANES

    def store_conv(grp, val):
        if grp < xs_groups:
            xs_ref[grp // per_xs, :, (grp % per_xs) * LANES:(grp % per_xs + 1) * LANES] = val
        elif grp < xs_groups + bm_ref.shape[0]:
            bm_ref[grp - xs_groups] = val
        else:
            cm_ref[grp - xs_groups - bm_ref.shape[0]] = val

    xb = x_ref[...].astype(BF16)
    hist = SUBLANES
    per = tm // SUBLANES
    gpc = cn // LANES

    @pl.when(i == 0)
    def _():
        halo_ref[:, 0:hist, :] = jnp.zeros((halo_ref.shape[0], hist, LANES), F32)

    @pl.when(i > 0)
    def _():
        halo_ref[:, 0:hist, :] = halo_ref[:, tm:tm + hist, :]

    for c in range(d_inner // cn):
        sl = slice(c * cn, (c + 1) * cn)
        z_ref[:, sl] = jnp.dot(xb, win_ref[:, sl], preferred_element_type=F32).astype(z_ref.dtype)
    for c in range(conv_dim // cn):
        u = jnp.dot(xb, win_ref[:, d_inner + c * cn:d_inner + (c + 1) * cn],
                    preferred_element_type=F32)
        for lg in range(gpc):
            halo_ref[c * gpc + lg, hist:hist + tm, :] = u[:, lg * LANES:(lg + 1) * LANES]
        for lg in range(gpc):
            grp = c * gpc + lg
            cols = slice(grp * LANES, (grp + 1) * LANES)
            taps = {back: halo_ref[grp, pl.ds(hist - back, per, stride=SUBLANES), :]
                    for back in range(1 - SUBLANES, SSD_CONV)}
            for p in range(SUBLANES):
                conv = cb_ref[:, cols] + jnp.zeros((per, LANES), F32)
                for k in range(SSD_CONV):
                    conv = conv + cw_ref[k:k + 1, cols] * taps[SSD_CONV - 1 - k - p]
                stage_ref[lg, pl.ds(p, per, stride=SUBLANES), :] = _silu(conv)
            store_conv(grp, stage_ref[lg].astype(xs_ref.dtype))
    dtt_raw = lax.dot_general(wdtt_ref[...], xb, NT_DIMS, preferred_element_type=F32)
    dtt_ref[...] = jax.nn.softplus(dtt_raw + dtbt_ref[...])


def ssd_in(x, w_in, conv_w, conv_b, dt_bias, *, d_inner, lead=(), tm=256, cn=512):
    bz, s, d = x.shape
    conv_dim, heads = conv_w.shape[1], dt_bias.shape[0]
    groups = SSD_GROUPS
    gw, n_state = d_inner // groups, (conv_dim - d_inner) // (2 * groups)
    kern = functools.partial(_ssd_in_kernel, tm=tm, cn=cn, d_inner=d_inner, conv_dim=conv_dim)
    grouped = lambda width: pl.BlockSpec((None, groups, tm, width), lambda b, i: (b, 0, i, 0))
    return pl.pallas_call(
        kern,
        out_shape=(jax.ShapeDtypeStruct((bz, s, d_inner), BF16),
                   jax.ShapeDtypeStruct((bz, groups, s, gw), BF16),
                   jax.ShapeDtypeStruct((bz, groups, s, n_state), BF16),
                   jax.ShapeDtypeStruct((bz, groups, s, n_state), BF16),
                   jax.ShapeDtypeStruct((bz, heads, s), F32)),
        grid=(bz, s // tm),
        in_specs=[pl.BlockSpec((None, tm, d), lambda b, i: (b, i, 0)),
                  _resident(w_in.shape, lead), _resident((heads, d)), _resident(conv_w.shape),
                  _resident((1, conv_dim)), _resident((heads, 1))],
        out_specs=(pl.BlockSpec((None, tm, d_inner), lambda b, i: (b, i, 0)),
                   grouped(gw), grouped(n_state), grouped(n_state),
                   pl.BlockSpec((None, heads, tm), lambda b, i: (b, 0, i))),
        scratch_shapes=[pltpu.VMEM((conv_dim // LANES, tm + SUBLANES, LANES), F32),
                        pltpu.VMEM((cn // LANES, tm, LANES), F32)],
        compiler_params=_params(("parallel", "arbitrary")),
        name="ssd_in",
    )(x, w_in, w_in[lead][:, d_inner + conv_dim:].T, conv_w, conv_b.reshape(1, conv_dim),
      dt_bias.reshape(heads, 1))


def _ssd_scan_kernel(xs_ref, b_ref, c_ref, dtt_ref, alog_ref, alogt_ref, dskip_ref, y_ref,
                     h_ref, *, cps):
    ck = SSD_CHUNK
    heads = dtt_ref.shape[0]
    groups, _, gw = h_ref.shape
    hpg = gw // HEAD_DIM
    per_block = LANES // HEAD_DIM

    @pl.when(pl.program_id(1) == 0)
    def _():
        h_ref[...] = jnp.zeros(h_ref.shape, F32)

    a2_row = -jnp.exp(alog_ref[...]) * LOG2_E
    a2_col = -jnp.exp(alogt_ref[...]) * LOG2_E
    ri = lax.broadcasted_iota(jnp.int32, (ck, ck), 0)
    ci = lax.broadcasted_iota(jnp.int32, (ck, ck), 1)
    causal = ci <= ri
    tril = causal.astype(F32)
    triu = (ri <= ci).astype(F32)
    lane = lax.broadcasted_iota(jnp.int32, (ck, gw), 1)
    lane1 = lax.broadcasted_iota(jnp.int32, (1, gw), 1)
    lane_b = lax.broadcasted_iota(jnp.int32, (ck, LANES), 1)
    head_mask = [(lane >= j * HEAD_DIM) & (lane < (j + 1) * HEAD_DIM) for j in range(hpg)]

    def pick_head(vals, lanes):
        out = vals[-1]
        for j in range(len(vals) - 2, -1, -1):
            out = jnp.where(lanes < (j + 1) * HEAD_DIM, vals[j], out)
        return out

    for q in range(cps):
        rows = slice(q * ck, (q + 1) * ck)
        dtt = dtt_ref[:, rows]
        acum = jnp.dot(tril, dtt.T * a2_row, precision=lax.Precision.HIGHEST,
                       preferred_element_type=F32)
        acumt = jnp.dot(dtt * a2_col, triu, precision=lax.Precision.HIGHEST,
                        preferred_element_type=F32)
        a_end = acumt[:, ck - 1:ck]
        wt = jnp.exp2(a_end - acumt) * dtt
        dec = jnp.broadcast_to(jnp.exp2(a_end), (heads, gw))
        for g in range(groups):
            xs_gb = xs_ref[g, rows, :]
            b_gb = b_ref[g, rows, :]
            c_gb = c_ref[g, rows, :]
            b_gt = b_gb.astype(F32).T
            cb = lax.dot_general(c_gb, b_gb, NT_DIMS, preferred_element_type=F32)
            h_g = h_ref[g]
            y_off = jnp.dot(c_gb, h_g.astype(BF16), preferred_element_type=F32)
            xs_heads = jnp.concatenate(
                [jnp.where(head_mask[j], xs_gb, jnp.zeros_like(xs_gb)) for j in range(hpg)], axis=0)
            m_parts, bw_parts, e_blocks = [], [], []
            for j in range(hpg):
                hd = g * hpg + j
                col = jnp.broadcast_to(acum[:, hd:hd + 1], (ck, ck))
                decay = jnp.exp2(jnp.where(causal, col - acumt[hd:hd + 1, :], -jnp.inf))
                m_parts.append((cb * decay * dtt[hd:hd + 1, :]).astype(BF16))
                bw_parts.append((b_gt * wt[hd:hd + 1, :]).astype(BF16))
                e_col = jnp.exp2(col)
                if j % per_block == 0:
                    e_blocks.append(e_col)
                else:
                    e_blocks[-1] = jnp.where(lane_b >= (j % per_block) * HEAD_DIM, e_col,
                                             e_blocks[-1])
            e_start = jnp.concatenate(e_blocks, axis=1)
            y_d = jnp.dot(jnp.concatenate(m_parts, axis=1), xs_heads,
                          preferred_element_type=F32)
            st = jnp.dot(jnp.concatenate(bw_parts, axis=1), xs_heads,
                         preferred_element_type=F32)
            y_g = y_d + y_off * e_start + dskip_ref[g] * xs_gb.astype(F32)
            y_ref[g, rows, :] = y_g.astype(y_ref.dtype)
            dec_g = dec[g * hpg:(g + 1) * hpg, :]
            h_ref[g] = pick_head([dec_g[j:j + 1, :] for j in range(hpg)], lane1) * h_g + st


def ssd_scan(xs, bm, cm, dtt, a_log, d_skip, *, cps=4):
    bz, groups, s, gw = xs.shape
    n_state = bm.shape[3]
    heads = a_log.shape[0]
    lt = cps * SSD_CHUNK
    seq = lambda width: pl.BlockSpec((None, groups, lt, width), lambda b, i: (b, 0, i, 0))
    return pl.pallas_call(
        functools.partial(_ssd_scan_kernel, cps=cps),
        out_shape=jax.ShapeDtypeStruct((bz, groups, s, gw), BF16),
        grid=(bz, s // lt),
        in_specs=[seq(gw), seq(n_state), seq(n_state),
                  pl.BlockSpec((None, heads, lt), lambda b, i: (b, 0, i)),
                  _resident((1, heads)), _resident((heads, 1)), _resident((groups, 1, gw))],
        out_specs=seq(gw),
        scratch_shapes=[pltpu.VMEM((groups, n_state, gw), F32)],
        compiler_params=_params(("parallel", "arbitrary")),
        name="ssd_scan",
    )(xs, bm, cm, dtt, a_log.reshape(1, heads), a_log.reshape(heads, 1),
      jnp.repeat(d_skip, HEAD_DIM).reshape(groups, 1, gw))


def _attn_kernel(*refs, dilations):
    nb = len(dilations)
    n = ATTN_WINDOW
    q_refs = refs[0:nb]
    kc_refs, kp_refs = refs[nb:3 * nb:2], refs[nb + 1:3 * nb:2]
    vc_refs, vp_refs = refs[3 * nb:5 * nb:2], refs[3 * nb + 1:5 * nb:2]
    brev_ref, bfar_ref, o_ref = refs[5 * nb:5 * nb + 3]
    scratch = refs[5 * nb + 3:]
    kw_refs, vw_refs = scratch[0:nb], scratch[nb:2 * nb]
    bias_scr, o_scr, l_scr, lg_scr, mx_scr = scratch[2 * nb:]
    first_tile = pl.program_id(2) == 0

    ri = lax.broadcasted_iota(jnp.int32, (n, n), 0)
    ki = lax.broadcasted_iota(jnp.int32, (n, n), 1)
    low_q = lax.broadcasted_iota(jnp.int32, (n, LANES), 1) < HEAD_DIM
    low_v = lax.broadcasted_iota(jnp.int32, (2 * n, LANES), 1) < HEAD_DIM

    for g in range(nb):
        for e in range(2):
            circ = pltpu.roll(jnp.broadcast_to(brev_ref[g, e:e + 1, :], (n, n)), 0, 1,
                              stride=1, stride_axis=0)
            far = bfar_ref[g, e:e + 1, :]
            current = jnp.where(ki <= ri, circ, NEG)
            bias_scr[g, e, 0, :, 0:n] = jnp.where(ki > ri, circ, jnp.where(ki == ri, far, NEG))
            bias_scr[g, e, 0, :, n:2 * n] = current
            bias_scr[g, e, 1, :, 0:n] = jnp.full((n, n), NEG, F32)
            bias_scr[g, e, 1, :, n:2 * n] = current

    for g in range(nb):
        kw_refs[g][:, 0:n, :] = kp_refs[g][...]
        kw_refs[g][:, n:, :] = kc_refs[g][...]
        vw_refs[g][:, 0:n, :] = vp_refs[g][...]
        vw_refs[g][:, n:, :] = vc_refs[g][...]

    one = jnp.ones((), BF16)
    zero = jnp.zeros((), BF16)
    units = ATTN_TILE // n

    def split(g, idx):
        blocks = units // dilations[g]
        c, j = idx // blocks, idx % blocks
        return c, j, pl.multiple_of(j * n, n)

    def logits_stage(g, idx):
        c, j, start = split(g, idx)
        q_u = q_refs[g][c, pl.ds(start, n), :]
        k_w = kw_refs[g][c, pl.ds(start, 2 * n), :]
        variant = jnp.logical_and(first_tile, j == 0).astype(jnp.int32)
        for e in range(2):
            own_q = low_q if e == 0 else ~low_q
            logits = lax.dot_general(jnp.where(own_q, q_u, zero), k_w, NT_DIMS,
                                     preferred_element_type=F32)
            logits = logits + bias_scr[g, e, variant]
            lg_scr[g % 2, idx, e] = logits
            mx_scr[g % 2, idx, e] = jnp.broadcast_to(jnp.max(logits, axis=-1, keepdims=True),
                                                     (n, LANES))

    def value_stage(g, idx):
        r = dilations[g]
        c, j, start = split(g, idx)
        v_w = vw_refs[g][c, pl.ds(start, 2 * n), :]
        res = []
        for e in range(2):
            m = mx_scr[g % 2, idx, e]
            p = jnp.concatenate(
                [jnp.exp2(lg_scr[g % 2, idx, e, :, half * n:(half + 1) * n] - m)
                 for half in range(2)], axis=-1).astype(BF16)
            res.append(jnp.dot(p, jnp.where(low_v if e == 0 else ~low_v, v_w, one),
                               preferred_element_type=F32))
        num = jnp.where(low_q, res[0], res[1])
        den = pltpu.roll(jnp.where(low_q, res[1], res[0]), HEAD_DIM, 1)
        rows = pl.ds(j * (n * r) + c, n, stride=r) if r > 1 else pl.ds(start, n)
        o_scr[g, rows, :] = num / den
        l_scr[g, rows, :] = (jnp.where(low_q, mx_scr[g % 2, idx, 0], mx_scr[g % 2, idx, 1])
                             + jnp.log2(den))

    for t in range(nb + 1):
        def body(idx, carry, t=t):
            if t > 0:
                value_stage(t - 1, idx)
            if t < nb:
                logits_stage(t, idx)
            return carry

        lax.fori_loop(0, units, body, 0, unroll=16)

    rb = 2 * n

    def mix(tb, carry):
        rows = pl.ds(pl.multiple_of(tb * rb, rb), rb)
        ls = [l_scr[g, rows, :] for g in range(nb)]
        top = functools.reduce(jnp.maximum, ls)
        ws = [jnp.exp2(l - top) for l in ls]
        acc = functools.reduce(lambda a, b: a + b, [w * o_scr[g, rows, :] for g, w in enumerate(ws)])
        o_ref[rows, :] = (acc / functools.reduce(lambda a, b: a + b, ws)).astype(o_ref.dtype)
        return carry

    lax.fori_loop(0, ATTN_TILE // rb, mix, 0)


def attn_mix(qs, ks, vs, brev, bfar, *, dilations):
    bz, groups, _, s, _ = qs[0].shape
    n, tile = ATTN_WINDOW, ATTN_TILE
    nb = len(dilations)

    def cur(r):
        return pl.BlockSpec((None, None, r, tile // r, LANES), lambda b, hg, i: (b, hg, 0, i, 0))

    def prev(r):
        per_tile = tile // r // n
        return pl.BlockSpec((None, None, r, n, LANES),
                            lambda b, hg, i: (b, hg, 0, jnp.maximum(i * per_tile - 1, 0), 0))

    in_specs = [cur(r) for r in dilations]
    args = list(qs)
    for arrs in (ks, vs):
        for a, r in zip(arrs, dilations):
            in_specs += [cur(r), prev(r)]
            args += [a, a]
    in_specs += [pl.BlockSpec((nb, None, 2, LANES), lambda b, hg, i: (0, hg, 0, 0)),
                 pl.BlockSpec((nb, None, 2, 1), lambda b, hg, i: (0, hg, 0, 0))]
    args += [brev, bfar]
    windows = [pltpu.VMEM((r, n + tile // r, LANES), BF16) for r in dilations]
    return pl.pallas_call(
        functools.partial(_attn_kernel, dilations=dilations),
        out_shape=jax.ShapeDtypeStruct((bz, groups, s, LANES), BF16),
        grid=(bz, groups, s // tile),
        in_specs=in_specs,
        out_specs=pl.BlockSpec((None, None, tile, LANES), lambda b, hg, i: (b, hg, i, 0)),
        scratch_shapes=windows + windows + [pltpu.VMEM((nb, 2, 2, n, 2 * n), F32),
                                            pltpu.VMEM((nb, tile, LANES), F32),
                                            pltpu.VMEM((nb, tile, LANES), F32),
                                            pltpu.VMEM((2, tile // n, 2, n, 2 * n), F32),
                                            pltpu.VMEM((2, tile // n, 2, n, LANES), F32)],
        compiler_params=_params(("parallel", "parallel", "arbitrary")),
        name="attn_mix",
    )(*args)


def _t5_causal_bucket(dist):
    max_exact = NUM_BUCKETS // 2
    logv = (jnp.log(jnp.maximum(dist, 1).astype(F32) / max_exact)
            / math.log(MAX_DISTANCE / max_exact))
    large = jnp.minimum(max_exact + (logv * (NUM_BUCKETS - max_exact)).astype(jnp.int32),
                        NUM_BUCKETS - 1)
    return jnp.where(dist < max_exact, dist, large)


def _bias_rows(rel_bias, heads):
    n = ATTN_WINDOW
    rev, far = [], []
    for branch, (window, r) in enumerate(ATTN_BRANCHES):
        buckets = _t5_causal_bucket(jnp.arange(n + 1, dtype=jnp.int32) * r)
        per_dist = rel_bias[buckets][:, branch * heads:(branch + 1) * heads].T.astype(F32) * LOG2_E
        rev.append(per_dist[:, (-jnp.arange(n)) % n])
        far.append(per_dist[:, n:n + 1])
    return (jnp.stack(rev).reshape(len(rev), heads // 2, 2, n),
            jnp.stack(far).reshape(len(far), heads // 2, 2, 1))


def kernel(x, ln_g, ln_b, ffn_w_in, ffn_w_out, m_in_proj, m_conv_w, m_conv_b, m_dt_bias, m_a_log,
           m_d, m_norm_w, m_out_proj, a_w_q, a_w_o, kv_w, rel_bias):
    bz, s, d = x.shape
    t = bz * s
    d_inner = m_out_proj.shape[1]
    conv_dim = m_conv_w.shape[2]
    width = a_w_o.shape[1]
    heads = width // HEAD_DIM
    dilations = tuple(r for _, r in ATTN_BRANCHES)
    nb = len(dilations)
    assert all(window // r == ATTN_WINDOW for window, r in ATTN_BRANCHES)
    x = x.reshape(t, d)
    ffn_w_in, ffn_w_out, m_in_proj, m_out_proj, a_w_q, a_w_o, kv_w = (
        w.astype(BF16) for w in (ffn_w_in, ffn_w_out, m_in_proj, m_out_proj, a_w_q, a_w_o, kv_w))
    ks = vs = brev = bfar = None
    for layer in range(DEPTH):
        x = ffn_ln(x, ffn_w_in, ffn_w_out, ln_g[layer, 0], ln_b[layer, 0], lead=(layer, 0))
        if layer < N_SSD_LAYERS:
            z, xs, bm, cm, dtt = ssd_in(x.reshape(bz, s, d), m_in_proj, m_conv_w[layer],
                                        m_conv_b[layer], m_dt_bias[layer], d_inner=d_inner,
                                        lead=(layer,))
            y = ssd_scan(xs, bm, cm, dtt, m_a_log[layer], m_d[layer])
            mixer = dict(ssd=(y, z.reshape(t, d_inner), m_norm_w[layer], m_out_proj, (layer,),
                              ln_g[layer, 1], ln_b[layer, 1]))
        else:
            a = layer - N_SSD_LAYERS
            qs = proj_cm(x.reshape(bz, s, d), a_w_q, dilations=dilations, width=width,
                         lead=(a,), scale=HEAD_DIM ** -0.5 * LOG2_E)
            o = attn_mix(qs, ks, vs, brev, bfar, dilations=dilations)
            mixer = dict(attn=(o, a_w_o, (a,), ln_g[layer, 1], ln_b[layer, 1]))
        x = ffn_ln(x, ffn_w_in, ffn_w_out, ln_g[layer, 2], ln_b[layer, 2], lead=(layer, 1),
                   **mixer)
        if layer == N_SSD_LAYERS - 1:
            kvs = proj_cm(x.reshape(bz, s, d), kv_w, dilations=dilations * 2, width=width)
            ks, vs = kvs[:nb], kvs[nb:]
            brev, bfar = _bias_rows(rel_bias, heads)
    return x.reshape(bz, s, d)
```

```python
import functools
import math

import jax
import jax.numpy as jnp
from jax import lax
from jax.experimental import pallas as pl
from jax.experimental.pallas import tpu as pltpu

F32 = jnp.float32
BF16 = jnp.bfloat16

DEPTH = 4
N_SSD_LAYERS = DEPTH // 2
ALPHA = (2.0 * DEPTH) ** 0.25
EPS = 1e-5
FFN_RES = 0.5
HEAD_DIM = 64
SSD_GROUPS = 8
SSD_STATE = 128
SSD_CHUNK = 128
SSD_CONV = 4
ATTN_BRANCHES = ((128, 1), (512, 4), (2048, 16))
ATTN_WINDOW = 128
ATTN_TILE = ATTN_WINDOW * max(r for _, r in ATTN_BRANCHES)
NUM_BUCKETS = 32
MAX_DISTANCE = 2048
NEG = -math.inf
LOG2_E = math.log2(math.e)

LANES = 128
SUBLANES = 8
VMEM_LIMIT_BYTES = 56 * 1024 * 1024

NT_DIMS = (((1,), (1,)), ((), ()))


def _params(semantics):
    return pltpu.CompilerParams(dimension_semantics=semantics,
                                vmem_limit_bytes=VMEM_LIMIT_BYTES)


def _resident(shape, lead=()):
    block = (None,) * len(lead) + tuple(shape[len(lead):])
    index = tuple(lead) + (0,) * (len(shape) - len(lead))
    return pl.BlockSpec(block, lambda *_: index, pipeline_mode=pl.Buffered(1))


def _layer_norm(y, g, b):
    mu = jnp.mean(y, axis=-1, keepdims=True)
    d = y - mu
    var = jnp.mean(d * d, axis=-1, keepdims=True)
    return d * lax.rsqrt(var + EPS) * g + b


def _silu(v):
    return v * jax.nn.sigmoid(v)


def _gated_norm_project(y_ref, z_ref, nw_ref, w_ref, rows):
    groups, _, gw = y_ref.shape
    acc = None
    for g in range(groups):
        sl = slice(g * gw, (g + 1) * gw)
        gated = y_ref[g, rows, :].astype(F32) * _silu(z_ref[rows, sl].astype(F32))
        ms = jnp.mean(gated * gated, axis=-1, keepdims=True)
        normed = (gated * lax.rsqrt(ms + EPS) * nw_ref[:, sl]).astype(BF16)
        part = jnp.dot(normed, w_ref[sl, :], preferred_element_type=F32)
        acc = part if acc is None else acc + part
    return acc


def _ffn_kernel(*refs, d_ff, ck, sub, mixer):
    if mixer == "ssd":
        x_ref, y_ref, z_ref, nw_ref, wm_ref, gm_ref, bm_ref = refs[:7]
    elif mixer == "attn":
        x_ref, a_ref, wm_ref, gm_ref, bm_ref = refs[:5]
    else:
        x_ref = refs[0]
    win_ref, wout_ref, g_ref, b_ref, o_ref = refs[-5:]
    for h in range(x_ref.shape[0] // sub):
        rows = slice(h * sub, (h + 1) * sub)
        x = x_ref[rows, :]
        if mixer == "ssd":
            mixed = _gated_norm_project(y_ref, z_ref, nw_ref, wm_ref, rows)
            x = _layer_norm(ALPHA * x + mixed, gm_ref[...], bm_ref[...])
        elif mixer == "attn":
            heads = jnp.concatenate([a_ref[lg, rows, :] for lg in range(a_ref.shape[0])], axis=-1)
            mixed = jnp.dot(heads, wm_ref[...], preferred_element_type=F32)
            x = _layer_norm(ALPHA * x + mixed, gm_ref[...], bm_ref[...])
        xb = x.astype(BF16)
        acc = jnp.zeros(x.shape, F32)
        for c in range(d_ff // ck):
            gate = jnp.dot(xb, win_ref[:, c * ck:(c + 1) * ck], preferred_element_type=F32)
            up = jnp.dot(xb, win_ref[:, d_ff + c * ck:d_ff + (c + 1) * ck],
                         preferred_element_type=F32)
            act = (_silu(gate) * up).astype(BF16)
            acc = acc + jnp.dot(act, wout_ref[c * ck:(c + 1) * ck, :],
                                preferred_element_type=F32)
        o_ref[rows, :] = _layer_norm(ALPHA * x + FFN_RES * acc, g_ref[...], b_ref[...])


def ffn_ln(x, w_in, w_out, g, b, *, lead=(), ssd=None, attn=None, ck=256, sub=512):
    t, d = x.shape
    d_ff = w_out.shape[-2]
    row = lambda v: v.reshape(1, -1)
    if ssd is not None:
        y, z, norm_w, w_proj, proj_lead, gm, bm = ssd
        tm, mixer = sub, "ssd"
        groups, per_batch, gw = y.shape[1], y.shape[2] // tm, y.shape[3]
        pre_specs = [pl.BlockSpec((None, groups, tm, gw),
                                  lambda i: (i // per_batch, 0, i % per_batch, 0)),
                     pl.BlockSpec((tm, z.shape[1]), lambda i: (i, 0)), _resident((1, z.shape[1])),
                     _resident(w_proj.shape, proj_lead), _resident((1, d)), _resident((1, d))]
        pre_args = [y, z, row(norm_w), w_proj, row(gm), row(bm)]
    elif attn is not None:
        heads, w_o, o_lead, gm, bm = attn
        tm, mixer = 2 * sub, "attn"
        groups, per_batch = heads.shape[1], heads.shape[2] // tm
        pre_specs = [pl.BlockSpec((None, groups, tm, LANES),
                                  lambda i: (i // per_batch, 0, i % per_batch, 0)),
                     _resident(w_o.shape, o_lead), _resident((1, d)), _resident((1, d))]
        pre_args = [heads, w_o, row(gm), row(bm)]
    else:
        tm, mixer, pre_specs, pre_args = 2 * sub, None, [], []
    return pl.pallas_call(
        functools.partial(_ffn_kernel, d_ff=d_ff, ck=ck, sub=sub, mixer=mixer),
        out_shape=jax.ShapeDtypeStruct((t, d), F32),
        grid=(t // tm,),
        in_specs=[pl.BlockSpec((tm, d), lambda i: (i, 0))] + pre_specs
                 + [_resident(w_in.shape, lead), _resident(w_out.shape, lead),
                    _resident((1, d)), _resident((1, d))],
        out_specs=pl.BlockSpec((tm, d), lambda i: (i, 0)),
        compiler_params=_params(("parallel",)),
        name="ffn_ln" if mixer is None else f"{mixer}_out_ffn_ln",
    )(x, *pre_args, w_in, w_out, row(g), row(b))


def _proj_cm_kernel(x_ref, w_ref, *refs, dilations, width, scale, sub):
    o_refs, scr = refs[:-1], refs[-1]
    for h in range(x_ref.shape[0] // sub):
        xb = x_ref[h * sub:(h + 1) * sub, :].astype(BF16)
        for n in sorted(range(len(dilations)), key=lambda n: -dilations[n]):
            o_ref, r = o_refs[n], dilations[n]
            out_rows = slice(h * sub // r, (h + 1) * sub // r)
            y = jnp.dot(xb, w_ref[:, n * width:(n + 1) * width],
                        preferred_element_type=F32) * scale
            if r == 1:
                for lg in range(width // LANES):
                    o_ref[lg, 0, out_rows, :] = y[:, lg * LANES:(lg + 1) * LANES].astype(BF16)
            else:
                for lg in range(width // LANES):
                    scr[h, lg] = y[:, lg * LANES:(lg + 1) * LANES]
                for lg in range(width // LANES):
                    for c in range(r):
                        o_ref[lg, c, out_rows, :] = (
                            scr[h, lg, pl.ds(c, sub // r, stride=r), :].astype(BF16))


def proj_cm(x, w, *, dilations, width, lead=(), scale=1.0, tm=1024, sub=1024):
    bz, s, d = x.shape
    groups = width // LANES
    kern = functools.partial(_proj_cm_kernel, dilations=dilations, width=width, scale=scale,
                             sub=sub)
    return pl.pallas_call(
        kern,
        out_shape=tuple(jax.ShapeDtypeStruct((bz, groups, r, s // r, LANES), BF16)
                        for r in dilations),
        grid=(bz, s // tm),
        in_specs=[pl.BlockSpec((None, tm, d), lambda b, i: (b, i, 0)), _resident(w.shape, lead)],
        out_specs=tuple(pl.BlockSpec((None, groups, r, tm // r, LANES),
                                     lambda b, i: (b, 0, 0, i, 0)) for r in dilations),
        scratch_shapes=[pltpu.VMEM((tm // sub, groups, sub, LANES), F32)],
        compiler_params=_params(("parallel", "parallel")),
        name="proj_cm",
    )(x, w)


def _ssd_in_kernel(x_ref, win_ref, wdtt_ref, cw_ref, cb_ref, dtbt_ref,
                   z_ref, xs_ref, bm_ref, cm_ref, dtt_ref, halo_ref, stage_ref,
                   *, tm, cn, d_inner, conv_dim):
    i = pl.program_id(1)
    xs_groups = d_inner // LANES
    per_xs = xs_ref.shape[2] // LANES

    def store_conv(grp, val):
        if grp < xs_groups:
            xs_ref[grp // per_xs, :, (grp % per_xs) * LANES:(grp % per_xs + 1) * LANES] = val
        elif grp < xs_groups + bm_ref.shape[0]:
            bm_ref[grp - xs_groups] = val
        else:
            cm_ref[grp - xs_groups - bm_ref.shape[0]] = val

    xb = x_ref[...].astype(BF16)
    hist = SUBLANES
    per = tm // SUBLANES
    gpc = cn // LANES

    @pl.when(i == 0)
    def _():
        halo_ref[:, 0:hist, :] = jnp.zeros((halo_ref.shape[0], hist, LANES), F32)

    @pl.when(i > 0)
    def _():
        halo_ref[:, 0:hist, :] = halo_ref[:, tm:tm + hist, :]

    for c in range(d_inner // cn):
        sl = slice(c * cn, (c + 1) * cn)
        z_ref[:, sl] = jnp.dot(xb, win_ref[:, sl], preferred_element_type=F32).astype(z_ref.dtype)
    for c in range(conv_dim // cn):
        u = jnp.dot(xb, win_ref[:, d_inner + c * cn:d_inner + (c + 1) * cn],
                    preferred_element_type=F32)
        for lg in range(gpc):
            halo_ref[c * gpc + lg, hist:hist + tm, :] = u[:, lg * LANES:(lg + 1) * LANES]
        for lg in range(gpc):
            grp = c * gpc + lg
            cols = slice(grp * LANES, (grp + 1) * LANES)
            taps = {back: halo_ref[grp, pl.ds(hist - back, per, stride=SUBLANES), :]
                    for back in range(1 - SUBLANES, SSD_CONV)}
            for p in range(SUBLANES):
                conv = cb_ref[:, cols] + jnp.zeros((per, LANES), F32)
                for k in range(SSD_CONV):
                    conv = conv + cw_ref[k:k + 1, cols] * taps[SSD_CONV - 1 - k - p]
                stage_ref[lg, pl.ds(p, per, stride=SUBLANES), :] = _silu(conv)
            store_conv(grp, stage_ref[lg].astype(xs_ref.dtype))
    dtt_raw = lax.dot_general(wdtt_ref[...], xb, NT_DIMS, preferred_element_type=F32)
    dtt_ref[...] = jax.nn.softplus(dtt_raw + dtbt_ref[...])


def ssd_in(x, w_in, conv_w, conv_b, dt_bias, *, d_inner, lead=(), tm=256, cn=512):
    bz, s, d = x.shape
    conv_dim, heads = conv_w.shape[1], dt_bias.shape[0]
    groups = SSD_GROUPS
    gw, n_state = d_inner // groups, (conv_dim - d_inner) // (2 * groups)
    kern = functools.partial(_ssd_in_kernel, tm=tm, cn=cn, d_inner=d_inner, conv_dim=conv_dim)
    grouped = lambda width: pl.BlockSpec((None, groups, tm, width), lambda b, i: (b, 0, i, 0))
    return pl.pallas_call(
        kern,
        out_shape=(jax.ShapeDtypeStruct((bz, s, d_inner), BF16),
                   jax.ShapeDtypeStruct((bz, groups, s, gw), BF16),
                   jax.ShapeDtypeStruct((bz, groups, s, n_state), BF16),
                   jax.ShapeDtypeStruct((bz, groups, s, n_state), BF16),
                   jax.ShapeDtypeStruct((bz, heads, s), F32)),
        grid=(bz, s // tm),
        in_specs=[pl.BlockSpec((None, tm, d), lambda b, i: (b, i, 0)),
                  _resident(w_in.shape, lead), _resident((heads, d)), _resident(conv_w.shape),
                  _resident((1, conv_dim)), _resident((heads, 1))],
        out_specs=(pl.BlockSpec((None, tm, d_inner), lambda b, i: (b, i, 0)),
                   grouped(gw), grouped(n_state), grouped(n_state),
                   pl.BlockSpec((None, heads, tm), lambda b, i: (b, 0, i))),
        scratch_shapes=[pltpu.VMEM((conv_dim // LANES, tm + SUBLANES, LANES), F32),
                        pltpu.VMEM((cn // LANES, tm, LANES), F32)],
        compiler_params=_params(("parallel", "arbitrary")),
        name="ssd_in",
    )(x, w_in, w_in[lead][:, d_inner + conv_dim:].T, conv_w, conv_b.reshape(1, conv_dim),
      dt_bias.reshape(heads, 1))


def _ssd_scan_kernel(xs_ref, b_ref, c_ref, dtt_ref, alog_ref, alogt_ref, dskip_ref, y_ref,
                     h_ref, *, cps):
    ck = SSD_CHUNK
    heads = dtt_ref.shape[0]
    groups, _, gw = h_ref.shape
    hpg = gw // HEAD_DIM
    per_block = LANES // HEAD_DIM

    @pl.when(pl.program_id(1) == 0)
    def _():
        h_ref[...] = jnp.zeros(h_ref.shape, F32)

    a2_row = -jnp.exp(alog_ref[...]) * LOG2_E
    a2_col = -jnp.exp(alogt_ref[...]) * LOG2_E
    ri = lax.broadcasted_iota(jnp.int32, (ck, ck), 0)
    ci = lax.broadcasted_iota(jnp.int32, (ck, ck), 1)
    causal = ci <= ri
    tril = causal.astype(F32)
    triu = (ri <= ci).astype(F32)
    lane = lax.broadcasted_iota(jnp.int32, (ck, gw), 1)
    lane1 = lax.broadcasted_iota(jnp.int32, (1, gw), 1)
    lane_b = lax.broadcasted_iota(jnp.int32, (ck, LANES), 1)
    head_mask = [(lane >= j * HEAD_DIM) & (lane < (j + 1) * HEAD_DIM) for j in range(hpg)]

    def pick_head(vals, lanes):
        out = vals[-1]
        for j in range(len(vals) - 2, -1, -1):
            out = jnp.where(lanes < (j + 1) * HEAD_DIM, vals[j], out)
        return out

    for q in range(cps):
        rows = slice(q * ck, (q + 1) * ck)
        dtt = dtt_ref[:, rows]
        acum = jnp.dot(tril, dtt.T * a2_row, precision=lax.Precision.HIGHEST,
                       preferred_element_type=F32)
        acumt = jnp.dot(dtt * a2_col, triu, precision=lax.Precision.HIGHEST,
                        preferred_element_type=F32)
        a_end = acumt[:, ck - 1:ck]
        wt = jnp.exp2(a_end - acumt) * dtt
        dec = jnp.broadcast_to(jnp.exp2(a_end), (heads, gw))
        for g in range(groups):
            xs_gb = xs_ref[g, rows, :]
            b_gb = b_ref[g, rows, :]
            c_gb = c_ref[g, rows, :]
            b_gt = b_gb.astype(F32).T
            cb = lax.dot_general(c_gb, b_gb, NT_DIMS, preferred_element_type=F32)
            h_g = h_ref[g]
            y_off = jnp.dot(c_gb, h_g.astype(BF16), preferred_element_type=F32)
            xs_heads = jnp.concatenate(
                [jnp.where(head_mask[j], xs_gb, jnp.zeros_like(xs_gb)) for j in range(hpg)], axis=0)
            m_parts, bw_parts, e_blocks = [], [], []
            for j in range(hpg):
                hd = g * hpg + j
                col = jnp.broadcast_to(acum[:, hd:hd + 1], (ck, ck))
                decay = jnp.exp2(jnp.where(causal, col - acumt[hd:hd + 1, :], -jnp.inf))
                m_parts.append((cb * decay * dtt[hd:hd + 1, :]).astype(BF16))
                bw_parts.append((b_gt * wt[hd:hd + 1, :]).astype(BF16))
                e_col = jnp.exp2(col)
                if j % per_block == 0:
                    e_blocks.append(e_col)
                else:
                    e_blocks[-1] = jnp.where(lane_b >= (j % per_block) * HEAD_DIM, e_col,
                                             e_blocks[-1])
            e_start = jnp.concatenate(e_blocks, axis=1)
            y_d = jnp.dot(jnp.concatenate(m_parts, axis=1), xs_heads,
                          preferred_element_type=F32)
            st = jnp.dot(jnp.concatenate(bw_parts, axis=1), xs_heads,
                         preferred_element_type=F32)
            y_g = y_d + y_off * e_start + dskip_ref[g] * xs_gb.astype(F32)
            y_ref[g, rows, :] = y_g.astype(y_ref.dtype)
            dec_g = dec[g * hpg:(g + 1) * hpg, :]
            h_ref[g] = pick_head([dec_g[j:j + 1, :] for j in range(hpg)], lane1) * h_g + st


def ssd_scan(xs, bm, cm, dtt, a_log, d_skip, *, cps=4):
    bz, groups, s, gw = xs.shape
    n_state = bm.shape[3]
    heads = a_log.shape[0]
    lt = cps * SSD_CHUNK
    seq = lambda width: pl.BlockSpec((None, groups, lt, width), lambda b, i: (b, 0, i, 0))
    return pl.pallas_call(
        functools.partial(_ssd_scan_kernel, cps=cps),
        out_shape=jax.ShapeDtypeStruct((bz, groups, s, gw), BF16),
        grid=(bz, s // lt),
        in_specs=[seq(gw), seq(n_state), seq(n_state),
                  pl.BlockSpec((None, heads, lt), lambda b, i: (b, 0, i)),
                  _resident((1, heads)), _resident((heads, 1)), _resident((groups, 1, gw))],
        out_specs=seq(gw),
        scratch_shapes=[pltpu.VMEM((groups, n_state, gw), F32)],
        compiler_params=_params(("parallel", "arbitrary")),
        name="ssd_scan",
    )(xs, bm, cm, dtt, a_log.reshape(1, heads), a_log.reshape(heads, 1),
      jnp.repeat(d_skip, HEAD_DIM).reshape(groups, 1, gw))


def _attn_kernel(*refs, dilations):
    nb = len(dilations)
    n = ATTN_WINDOW
    q_refs = refs[0:nb]
    kc_refs, kp_refs = refs[nb:3 * nb:2], refs[nb + 1:3 * nb:2]
    vc_refs, vp_refs = refs[3 * nb:5 * nb:2], refs[3 * nb + 1:5 * nb:2]
    brev_ref, bfar_ref, o_ref = refs[5 * nb:5 * nb + 3]
    scratch = refs[5 * nb + 3:]
    kw_refs, vw_refs = scratch[0:nb], scratch[nb:2 * nb]
    bias_scr, o_scr, l_scr, lg_scr, mx_scr = scratch[2 * nb:]
    first_tile = pl.program_id(2) == 0

    ri = lax.broadcasted_iota(jnp.int32, (n, n), 0)
    ki = lax.broadcasted_iota(jnp.int32, (n, n), 1)
    low_q = lax.broadcasted_iota(jnp.int32, (n, LANES), 1) < HEAD_DIM
    low_v = lax.broadcasted_iota(jnp.int32, (2 * n, LANES), 1) < HEAD_DIM

    @pl.when(first_tile)
    def _():
        for g in range(nb):
            for e in range(2):
                circ = pltpu.roll(jnp.broadcast_to(brev_ref[g, e:e + 1, :], (n, n)), 0, 1,
                                  stride=1, stride_axis=0)
                far = bfar_ref[g, e:e + 1, :]
                current = jnp.where(ki <= ri, circ, NEG)
                bias_scr[g, e, 0, :, 0:n] = jnp.where(ki > ri, circ,
                                                      jnp.where(ki == ri, far, NEG))
                bias_scr[g, e, 0, :, n:2 * n] = current
                bias_scr[g, e, 1, :, 0:n] = jnp.full((n, n), NEG, F32)
                bias_scr[g, e, 1, :, n:2 * n] = current

    for g in range(nb):
        kw_refs[g][:, 0:n, :] = kp_refs[g][...]
        kw_refs[g][:, n:, :] = kc_refs[g][...]
        vw_refs[g][:, 0:n, :] = vp_refs[g][...]
        vw_refs[g][:, n:, :] = vc_refs[g][...]

    one = jnp.ones((), BF16)
    zero = jnp.zeros((), BF16)
    units = ATTN_TILE // n

    def split(g, idx):
        blocks = units // dilations[g]
        c, j = idx // blocks, idx % blocks
        return c, j, pl.multiple_of(j * n, n)

    def logits_stage(g, idx):
        c, j, start = split(g, idx)
        q_u = q_refs[g][c, pl.ds(start, n), :]
        k_w = kw_refs[g][c, pl.ds(start, 2 * n), :]
        variant = jnp.logical_and(first_tile, j == 0).astype(jnp.int32)
        for e in range(2):
            own_q = low_q if e == 0 else ~low_q
            logits = lax.dot_general(jnp.where(own_q, q_u, zero), k_w, NT_DIMS,
                                     preferred_element_type=F32)
            logits = logits + bias_scr[g, e, variant]
            lg_scr[g % 2, idx, e] = logits
            mx_scr[g % 2, idx, e] = jnp.broadcast_to(jnp.max(logits, axis=-1, keepdims=True),
                                                     (n, LANES))

    def value_stage(g, idx):
        r = dilations[g]
        c, j, start = split(g, idx)
        v_w = vw_refs[g][c, pl.ds(start, 2 * n), :]
        res = []
        for e in range(2):
            m = mx_scr[g % 2, idx, e]
            p = jnp.concatenate(
                [jnp.exp2(lg_scr[g % 2, idx, e, :, half * n:(half + 1) * n] - m)
                 for half in range(2)], axis=-1).astype(BF16)
            res.append(jnp.dot(p, jnp.where(low_v if e == 0 else ~low_v, v_w, one),
                               preferred_element_type=F32))
        num = jnp.where(low_q, res[0], res[1])
        den = pltpu.roll(jnp.where(low_q, res[1], res[0]), HEAD_DIM, 1)
        rows = pl.ds(j * (n * r) + c, n, stride=r) if r > 1 else pl.ds(start, n)
        o_scr[g, rows, :] = num / den
        l_scr[g, rows, :] = (jnp.where(low_q, mx_scr[g % 2, idx, 0], mx_scr[g % 2, idx, 1])
                             + jnp.log2(den))

    for t in range(nb + 1):
        def body(idx, carry, t=t):
            if t > 0:
                value_stage(t - 1, idx)
            if t < nb:
                logits_stage(t, idx)
            return carry

        lax.fori_loop(0, units, body, 0, unroll=16)

    rb = 2 * n

    def mix(tb, carry):
        rows = pl.ds(pl.multiple_of(tb * rb, rb), rb)
        ls = [l_scr[g, rows, :] for g in range(nb)]
        top = functools.reduce(jnp.maximum, ls)
        ws = [jnp.exp2(l - top) for l in ls]
        acc = functools.reduce(lambda a, b: a + b, [w * o_scr[g, rows, :] for g, w in enumerate(ws)])
        o_ref[rows, :] = (acc / functools.reduce(lambda a, b: a + b, ws)).astype(o_ref.dtype)
        return carry

    lax.fori_loop(0, ATTN_TILE // rb, mix, 0)


def attn_mix(qs, ks, vs, brev, bfar, *, dilations):
    bz, groups, _, s, _ = qs[0].shape
    n, tile = ATTN_WINDOW, ATTN_TILE
    nb = len(dilations)

    def cur(r):
        return pl.BlockSpec((None, None, r, tile // r, LANES), lambda b, hg, i: (b, hg, 0, i, 0))

    def prev(r):
        per_tile = tile // r // n
        return pl.BlockSpec((None, None, r, n, LANES),
                            lambda b, hg, i: (b, hg, 0, jnp.maximum(i * per_tile - 1, 0), 0))

    in_specs = [cur(r) for r in dilations]
    args = list(qs)
    for arrs in (ks, vs):
        for a, r in zip(arrs, dilations):
            in_specs += [cur(r), prev(r)]
            args += [a, a]
    in_specs += [pl.BlockSpec((nb, None, 2, LANES), lambda b, hg, i: (0, hg, 0, 0)),
                 pl.BlockSpec((nb, None, 2, 1), lambda b, hg, i: (0, hg, 0, 0))]
    args += [brev, bfar]
    windows = [pltpu.VMEM((r, n + tile // r, LANES), BF16) for r in dilations]
    return pl.pallas_call(
        functools.partial(_attn_kernel, dilations=dilations),
        out_shape=jax.ShapeDtypeStruct((bz, groups, s, LANES), BF16),
        grid=(bz, groups, s // tile),
        in_specs=in_specs,
        out_specs=pl.BlockSpec((None, None, tile, LANES), lambda b, hg, i: (b, hg, i, 0)),
        scratch_shapes=windows + windows + [pltpu.VMEM((nb, 2, 2, n, 2 * n), F32),
                                            pltpu.VMEM((nb, tile, LANES), F32),
                                            pltpu.VMEM((nb, tile, LANES), F32),
                                            pltpu.VMEM((2, tile // n, 2, n, 2 * n), F32),
                                            pltpu.VMEM((2, tile // n, 2, n, LANES), F32)],
        compiler_params=_params(("parallel", "parallel", "arbitrary")),
        name="attn_mix",
    )(*args)


def _t5_causal_bucket(dist):
    max_exact = NUM_BUCKETS // 2
    logv = (jnp.log(jnp.maximum(dist, 1).astype(F32) / max_exact)
            / math.log(MAX_DISTANCE / max_exact))
    large = jnp.minimum(max_exact + (logv * (NUM_BUCKETS - max_exact)).astype(jnp.int32),
                        NUM_BUCKETS - 1)
    return jnp.where(dist < max_exact, dist, large)


def _bias_rows(rel_bias, heads):
    n = ATTN_WINDOW
    rev, far = [], []
    for branch, (window, r) in enumerate(ATTN_BRANCHES):
        buckets = _t5_causal_bucket(jnp.arange(n + 1, dtype=jnp.int32) * r)
        per_dist = rel_bias[buckets][:, branch * heads:(branch + 1) * heads].T.astype(F32) * LOG2_E
        rev.append(per_dist[:, (-jnp.arange(n)) % n])
        far.append(per_dist[:, n:n + 1])
    return (jnp.stack(rev).reshape(len(rev), heads // 2, 2, n),
            jnp.stack(far).reshape(len(far), heads // 2, 2, 1))


def kernel(x, ln_g, ln_b, ffn_w_in, ffn_w_out, m_in_proj, m_conv_w, m_conv_b, m_dt_bias, m_a_log,
           m_d, m_norm_w, m_out_proj, a_w_q, a_w_o, kv_w, rel_bias):
    bz, s, d = x.shape
    t = bz * s
    d_inner = m_out_proj.shape[1]
    conv_dim = m_conv_w.shape[2]
    width = a_w_o.shape[1]
    heads = width // HEAD_DIM
    dilations = tuple(r for _, r in ATTN_BRANCHES)
    nb = len(dilations)
    assert all(window // r == ATTN_WINDOW for window, r in ATTN_BRANCHES)
    x = x.reshape(t, d)
    ffn_w_in, ffn_w_out, m_in_proj, m_out_proj, a_w_q, a_w_o, kv_w = (
        w.astype(BF16) for w in (ffn_w_in, ffn_w_out, m_in_proj, m_out_proj, a_w_q, a_w_o, kv_w))
    ks = vs = brev = bfar = None
    for layer in range(DEPTH):
        x = ffn_ln(x, ffn_w_in, ffn_w_out, ln_g[layer, 0], ln_b[layer, 0], lead=(layer, 0))
        if layer < N_SSD_LAYERS:
            z, xs, bm, cm, dtt = ssd_in(x.reshape(bz, s, d), m_in_proj, m_conv_w[layer],
                                        m_conv_b[layer], m_dt_bias[layer], d_inner=d_inner,
                                        lead=(layer,))
            y = ssd_scan(xs, bm, cm, dtt, m_a_log[layer], m_d[layer])
            mixer = dict(ssd=(y, z.reshape(t, d_inner), m_norm_w[layer], m_out_proj, (layer,),
                              ln_g[layer, 1], ln_b[layer, 1]))
        else:
            a = layer - N_SSD_LAYERS
            qs = proj_cm(x.reshape(bz, s, d), a_w_q, dilations=dilations, width=width,
                         lead=(a,), scale=HEAD_DIM ** -0.5 * LOG2_E)
            o = attn_mix(qs, ks, vs, brev, bfar, dilations=dilations)
            mixer = dict(attn=(o, a_w_o, (a,), ln_g[layer, 1], ln_b[layer, 1]))
        x = ffn_ln(x, ffn_w_in, ffn_w_out, ln_g[layer, 2], ln_b[layer, 2], lead=(layer, 1),
                   **mixer)
        if layer == N_SSD_LAYERS - 1:
            kvs = proj_cm(x.reshape(bz, s, d), kv_w, dilations=dilations * 2, width=width)
            ks, vs = kvs[:nb], kvs[nb:]
            brev, bfar = _bias_rows(rel_bias, heads)
    return x.reshape(bz, s, d)
```

```python
import functools
import math

import jax
import jax.numpy as jnp
from jax import lax
from jax.experimental import pallas as pl
from jax.experimental.pallas import tpu as pltpu

F32 = jnp.float32
BF16 = jnp.bfloat16

DEPTH = 4
N_SSD_LAYERS = DEPTH // 2
ALPHA = (2.0 * DEPTH) ** 0.25
EPS = 1e-5
FFN_RES = 0.5
HEAD_DIM = 64
SSD_GROUPS = 8
SSD_STATE = 128
SSD_CHUNK = 128
SSD_CONV = 4
ATTN_BRANCHES = ((128, 1), (512, 4), (2048, 16))
ATTN_WINDOW = 128
ATTN_TILE = ATTN_WINDOW * max(r for _, r in ATTN_BRANCHES)
NUM_BUCKETS = 32
MAX_DISTANCE = 2048
NEG = -math.inf
LOG2_E = math.log2(math.e)

LANES = 128
SUBLANES = 8
VMEM_LIMIT_BYTES = 56 * 1024 * 1024

NT_DIMS = (((1,), (1,)), ((), ()))


def _params(semantics):
    return pltpu.CompilerParams(dimension_semantics=semantics,
                                vmem_limit_bytes=VMEM_LIMIT_BYTES)


def _resident(shape, lead=()):
    block = (None,) * len(lead) + tuple(shape[len(lead):])
    index = tuple(lead) + (0,) * (len(shape) - len(lead))
    return pl.BlockSpec(block, lambda *_: index, pipeline_mode=pl.Buffered(1))


def _layer_norm(y, g, b):
    mu = jnp.mean(y, axis=-1, keepdims=True)
    d = y - mu
    var = jnp.mean(d * d, axis=-1, keepdims=True)
    return d * lax.rsqrt(var + EPS) * g + b


def _silu(v):
    return v * jax.nn.sigmoid(v)


def _gated_norm_project(y_ref, z_ref, nw_ref, w_ref, rows):
    groups, _, gw = y_ref.shape
    acc = None
    for g in range(groups):
        sl = slice(g * gw, (g + 1) * gw)
        gated = y_ref[g, rows, :].astype(F32) * _silu(z_ref[rows, sl].astype(F32))
        ms = jnp.mean(gated * gated, axis=-1, keepdims=True)
        normed = (gated * lax.rsqrt(ms + EPS) * nw_ref[:, sl]).astype(BF16)
        part = jnp.dot(normed, w_ref[sl, :], preferred_element_type=F32)
        acc = part if acc is None else acc + part
    return acc


def _ffn_kernel(*refs, d_ff, ck, sub, mixer):
    if mixer == "ssd":
        x_ref, y_ref, z_ref, nw_ref, wm_ref, gm_ref, bm_ref = refs[:7]
    elif mixer == "attn":
        x_ref, a_ref, wm_ref, gm_ref, bm_ref = refs[:5]
    else:
        x_ref = refs[0]
    win_ref, wout_ref, g_ref, b_ref, o_ref = refs[-5:]
    for h in range(x_ref.shape[0] // sub):
        rows = slice(h * sub, (h + 1) * sub)
        x = x_ref[rows, :]
        if mixer == "ssd":
            mixed = _gated_norm_project(y_ref, z_ref, nw_ref, wm_ref, rows)
            x = _layer_norm(ALPHA * x + mixed, gm_ref[...], bm_ref[...])
        elif mixer == "attn":
            heads = jnp.concatenate([a_ref[lg, rows, :] for lg in range(a_ref.shape[0])], axis=-1)
            mixed = jnp.dot(heads, wm_ref[...], preferred_element_type=F32)
            x = _layer_norm(ALPHA * x + mixed, gm_ref[...], bm_ref[...])
        xb = x.astype(BF16)
        acc = jnp.zeros(x.shape, F32)
        for c in range(d_ff // ck):
            gate = jnp.dot(xb, win_ref[:, c * ck:(c + 1) * ck], preferred_element_type=F32)
            up = jnp.dot(xb, win_ref[:, d_ff + c * ck:d_ff + (c + 1) * ck],
                         preferred_element_type=F32)
            act = (_silu(gate) * up).astype(BF16)
            acc = acc + jnp.dot(act, wout_ref[c * ck:(c + 1) * ck, :],
                                preferred_element_type=F32)
        o_ref[rows, :] = _layer_norm(ALPHA * x + FFN_RES * acc, g_ref[...], b_ref[...])


def ffn_ln(x, w_in, w_out, g, b, *, lead=(), ssd=None, attn=None, ck=256, sub=512):
    t, d = x.shape
    d_ff = w_out.shape[-2]
    row = lambda v: v.reshape(1, -1)
    if ssd is not None:
        y, z, norm_w, w_proj, proj_lead, gm, bm = ssd
        tm, mixer = sub, "ssd"
        groups, per_batch, gw = y.shape[1], y.shape[2] // tm, y.shape[3]
        pre_specs = [pl.BlockSpec((None, groups, tm, gw),
                                  lambda i: (i // per_batch, 0, i % per_batch, 0)),
                     pl.BlockSpec((tm, z.shape[1]), lambda i: (i, 0)), _resident((1, z.shape[1])),
                     _resident(w_proj.shape, proj_lead), _resident((1, d)), _resident((1, d))]
        pre_args = [y, z, row(norm_w), w_proj, row(gm), row(bm)]
    elif attn is not None:
        heads, w_o, o_lead, gm, bm = attn
        tm, mixer = 2 * sub, "attn"
        groups, per_batch = heads.shape[1], heads.shape[2] // tm
        pre_specs = [pl.BlockSpec((None, groups, tm, LANES),
                                  lambda i: (i // per_batch, 0, i % per_batch, 0)),
                     _resident(w_o.shape, o_lead), _resident((1, d)), _resident((1, d))]
        pre_args = [heads, w_o, row(gm), row(bm)]
    else:
        tm, mixer, pre_specs, pre_args = 2 * sub, None, [], []
    return pl.pallas_call(
        functools.partial(_ffn_kernel, d_ff=d_ff, ck=ck, sub=sub, mixer=mixer),
        out_shape=jax.ShapeDtypeStruct((t, d), F32),
        grid=(t // tm,),
        in_specs=[pl.BlockSpec((tm, d), lambda i: (i, 0))] + pre_specs
                 + [_resident(w_in.shape, lead), _resident(w_out.shape, lead),
                    _resident((1, d)), _resident((1, d))],
        out_specs=pl.BlockSpec((tm, d), lambda i: (i, 0)),
        compiler_params=_params(("parallel",)),
        name="ffn_ln" if mixer is None else f"{mixer}_out_ffn_ln",
    )(x, *pre_args, w_in, w_out, row(g), row(b))


def _proj_cm_kernel(x_ref, w_ref, *refs, dilations, width, scale, sub):
    o_refs, scr = refs[:-1], refs[-1]
    for h in range(x_ref.shape[0] // sub):
        xb = x_ref[h * sub:(h + 1) * sub, :].astype(BF16)
        for n in sorted(range(len(dilations)), key=lambda n: -dilations[n]):
            o_ref, r = o_refs[n], dilations[n]
            out_rows = slice(h * sub // r, (h + 1) * sub // r)
            y = jnp.dot(xb, w_ref[:, n * width:(n + 1) * width],
                        preferred_element_type=F32) * scale
            if r == 1:
                for lg in range(width // LANES):
                    o_ref[lg, 0, out_rows, :] = y[:, lg * LANES:(lg + 1) * LANES].astype(BF16)
            else:
                for lg in range(width // LANES):
                    scr[h, lg] = y[:, lg * LANES:(lg + 1) * LANES]
                for lg in range(width // LANES):
                    for c in range(r):
                        o_ref[lg, c, out_rows, :] = (
                            scr[h, lg, pl.ds(c, sub // r, stride=r), :].astype(BF16))


def proj_cm(x, w, *, dilations, width, lead=(), scale=1.0, tm=1024, sub=1024):
    bz, s, d = x.shape
    groups = width // LANES
    kern = functools.partial(_proj_cm_kernel, dilations=dilations, width=width, scale=scale,
                             sub=sub)
    return pl.pallas_call(
        kern,
        out_shape=tuple(jax.ShapeDtypeStruct((bz, groups, r, s // r, LANES), BF16)
                        for r in dilations),
        grid=(bz, s // tm),
        in_specs=[pl.BlockSpec((None, tm, d), lambda b, i: (b, i, 0)), _resident(w.shape, lead)],
        out_specs=tuple(pl.BlockSpec((None, groups, r, tm // r, LANES),
                                     lambda b, i: (b, 0, 0, i, 0)) for r in dilations),
        scratch_shapes=[pltpu.VMEM((tm // sub, groups, sub, LANES), F32)],
        compiler_params=_params(("parallel", "parallel")),
        name="proj_cm",
    )(x, w)


def _ssd_in_kernel(x_ref, win_ref, wdtt_ref, cw_ref, cb_ref, dtbt_ref,
                   z_ref, xs_ref, bm_ref, cm_ref, dtt_ref, halo_ref, stage_ref,
                   *, tm, cn, d_inner, conv_dim):
    i = pl.program_id(1)
    xs_groups = d_inner // LANES
    per_xs = xs_ref.shape[2] // LANES

    def store_conv(grp, val):
        if grp < xs_groups:
            xs_ref[grp // per_xs, :, (grp % per_xs) * LANES:(grp % per_xs + 1) * LANES] = val
        elif grp < xs_groups + bm_ref.shape[0]:
            bm_ref[grp - xs_groups] = val
        else:
            cm_ref[grp - xs_groups - bm_ref.shape[0]] = val

    xb = x_ref[...].astype(BF16)
    hist = SUBLANES
    per = tm // SUBLANES
    gpc = cn // LANES

    @pl.when(i == 0)
    def _():
        halo_ref[:, 0:hist, :] = jnp.zeros((halo_ref.shape[0], hist, LANES), F32)

    @pl.when(i > 0)
    def _():
        halo_ref[:, 0:hist, :] = halo_ref[:, tm:tm + hist, :]

    for c in range(d_inner // cn):
        sl = slice(c * cn, (c + 1) * cn)
        z_ref[:, sl] = jnp.dot(xb, win_ref[:, sl], preferred_element_type=F32).astype(z_ref.dtype)
    for c in range(conv_dim // cn):
        u = jnp.dot(xb, win_ref[:, d_inner + c * cn:d_inner + (c + 1) * cn],
                    preferred_element_type=F32)
        for lg in range(gpc):
            halo_ref[c * gpc + lg, hist:hist + tm, :] = u[:, lg * LANES:(lg + 1) * LANES]
        for lg in range(gpc):
            grp = c * gpc + lg
            cols = slice(grp * LANES, (grp + 1) * LANES)
            taps = {back: halo_ref[grp, pl.ds(hist - back, per, stride=SUBLANES), :]
                    for back in range(1 - SUBLANES, SSD_CONV)}
            for p in range(SUBLANES):
                conv = cb_ref[:, cols] + jnp.zeros((per, LANES), F32)
                for k in range(SSD_CONV):
                    conv = conv + cw_ref[k:k + 1, cols] * taps[SSD_CONV - 1 - k - p]
                stage_ref[lg, pl.ds(p, per, stride=SUBLANES), :] = _silu(conv)
            store_conv(grp, stage_ref[lg].astype(xs_ref.dtype))
    dtt_raw = lax.dot_general(wdtt_ref[...], xb, NT_DIMS, preferred_element_type=F32)
    dtt_ref[...] = jax.nn.softplus(dtt_raw + dtbt_ref[...])


def ssd_in(x, w_in, conv_w, conv_b, dt_bias, *, d_inner, lead=(), tm=256, cn=512):
    bz, s, d = x.shape
    conv_dim, heads = conv_w.shape[1], dt_bias.shape[0]
    groups = SSD_GROUPS
    gw, n_state = d_inner // groups, (conv_dim - d_inner) // (2 * groups)
    kern = functools.partial(_ssd_in_kernel, tm=tm, cn=cn, d_inner=d_inner, conv_dim=conv_dim)
    grouped = lambda width: pl.BlockSpec((None, groups, tm, width), lambda b, i: (b, 0, i, 0))
    return pl.pallas_call(
        kern,
        out_shape=(jax.ShapeDtypeStruct((bz, s, d_inner), BF16),
                   jax.ShapeDtypeStruct((bz, groups, s, gw), BF16),
                   jax.ShapeDtypeStruct((bz, groups, s, n_state), BF16),
                   jax.ShapeDtypeStruct((bz, groups, s, n_state), BF16),
                   jax.ShapeDtypeStruct((bz, heads, s), F32)),
        grid=(bz, s // tm),
        in_specs=[pl.BlockSpec((None, tm, d), lambda b, i: (b, i, 0)),
                  _resident(w_in.shape, lead), _resident((heads, d)), _resident(conv_w.shape),
                  _resident((1, conv_dim)), _resident((heads, 1))],
        out_specs=(pl.BlockSpec((None, tm, d_inner), lambda b, i: (b, i, 0)),
                   grouped(gw), grouped(n_state), grouped(n_state),
                   pl.BlockSpec((None, heads, tm), lambda b, i: (b, 0, i))),
        scratch_shapes=[pltpu.VMEM((conv_dim // LANES, tm + SUBLANES, LANES), F32),
                        pltpu.VMEM((cn // LANES, tm, LANES), F32)],
        compiler_params=_params(("parallel", "arbitrary")),
        name="ssd_in",
    )(x, w_in, w_in[lead][:, d_inner + conv_dim:].T, conv_w, conv_b.reshape(1, conv_dim),
      dt_bias.reshape(heads, 1))


def _ssd_scan_kernel(xs_ref, b_ref, c_ref, dtt_ref, alog_ref, alogt_ref, dskip_ref, y_ref,
                     h_ref, *, cps):
    ck = SSD_CHUNK
    heads = dtt_ref.shape[0]
    groups, _, gw = h_ref.shape
    hpg = gw // HEAD_DIM
    per_block = LANES // HEAD_DIM

    @pl.when(pl.program_id(1) == 0)
    def _():
        h_ref[...] = jnp.zeros(h_ref.shape, F32)

    a2_row = -jnp.exp(alog_ref[...]) * LOG2_E
    a2_col = -jnp.exp(alogt_ref[...]) * LOG2_E
    ri = lax.broadcasted_iota(jnp.int32, (ck, ck), 0)
    ci = lax.broadcasted_iota(jnp.int32, (ck, ck), 1)
    causal = ci <= ri
    tril = causal.astype(F32)
    triu = (ri <= ci).astype(F32)
    lane = lax.broadcasted_iota(jnp.int32, (ck, gw), 1)
    lane1 = lax.broadcasted_iota(jnp.int32, (1, gw), 1)
    lane_b = lax.broadcasted_iota(jnp.int32, (ck, LANES), 1)
    head_mask = [(lane >= j * HEAD_DIM) & (lane < (j + 1) * HEAD_DIM) for j in range(hpg)]

    def pick_head(vals, lanes):
        out = vals[-1]
        for j in range(len(vals) - 2, -1, -1):
            out = jnp.where(lanes < (j + 1) * HEAD_DIM, vals[j], out)
        return out

    for q in range(cps):
        rows = slice(q * ck, (q + 1) * ck)
        dtt = dtt_ref[:, rows]
        acum = jnp.dot(tril, dtt.T * a2_row, precision=lax.Precision.HIGHEST,
                       preferred_element_type=F32)
        acumt = jnp.dot(dtt * a2_col, triu, precision=lax.Precision.HIGHEST,
                        preferred_element_type=F32)
        a_end = acumt[:, ck - 1:ck]
        wt = jnp.exp2(a_end - acumt) * dtt
        dec = jnp.broadcast_to(jnp.exp2(a_end), (heads, gw))
        for g in range(groups):
            xs_gb = xs_ref[g, rows, :]
            b_gb = b_ref[g, rows, :]
            c_gb = c_ref[g, rows, :]
            b_gt = b_gb.astype(F32).T
            cb = lax.dot_general(c_gb, b_gb, NT_DIMS, preferred_element_type=F32)
            h_g = h_ref[g]
            y_off = jnp.dot(c_gb, h_g.astype(BF16), preferred_element_type=F32)
            xs_heads = jnp.concatenate(
                [jnp.where(head_mask[j], xs_gb, jnp.zeros_like(xs_gb)) for j in range(hpg)], axis=0)
            m_parts, bw_parts, e_blocks = [], [], []
            for j in range(hpg):
                hd = g * hpg + j
                col = jnp.broadcast_to(acum[:, hd:hd + 1], (ck, ck))
                decay = jnp.exp2(jnp.where(causal, col - acumt[hd:hd + 1, :], -jnp.inf))
                m_parts.append((cb * decay * dtt[hd:hd + 1, :]).astype(BF16))
                bw_parts.append((b_gt * wt[hd:hd + 1, :]).astype(BF16))
                e_col = jnp.exp2(col)
                if j % per_block == 0:
                    e_blocks.append(e_col)
                else:
                    e_blocks[-1] = jnp.where(lane_b >= (j % per_block) * HEAD_DIM, e_col,
                                             e_blocks[-1])
            e_start = jnp.concatenate(e_blocks, axis=1)
            y_d = jnp.dot(jnp.concatenate(m_parts, axis=1), xs_heads,
                          preferred_element_type=F32)
            st = jnp.dot(jnp.concatenate(bw_parts, axis=1), xs_heads,
                         preferred_element_type=F32)
            y_g = y_d + y_off * e_start + dskip_ref[g] * xs_gb.astype(F32)
            y_ref[g, rows, :] = y_g.astype(y_ref.dtype)
            dec_g = dec[g * hpg:(g + 1) * hpg, :]
            h_ref[g] = pick_head([dec_g[j:j + 1, :] for j in range(hpg)], lane1) * h_g + st


def ssd_scan(xs, bm, cm, dtt, a_log, d_skip, *, cps=4):
    bz, groups, s, gw = xs.shape
    n_state = bm.shape[3]
    heads = a_log.shape[0]
    lt = cps * SSD_CHUNK
    seq = lambda width: pl.BlockSpec((None, groups, lt, width), lambda b, i: (b, 0, i, 0))
    return pl.pallas_call(
        functools.partial(_ssd_scan_kernel, cps=cps),
        out_shape=jax.ShapeDtypeStruct((bz, groups, s, gw), BF16),
        grid=(bz, s // lt),
        in_specs=[seq(gw), seq(n_state), seq(n_state),
                  pl.BlockSpec((None, heads, lt), lambda b, i: (b, 0, i)),
                  _resident((1, heads)), _resident((heads, 1)), _resident((groups, 1, gw))],
        out_specs=seq(gw),
        scratch_shapes=[pltpu.VMEM((groups, n_state, gw), F32)],
        compiler_params=_params(("parallel", "arbitrary")),
        name="ssd_scan",
    )(xs, bm, cm, dtt, a_log.reshape(1, heads), a_log.reshape(heads, 1),
      jnp.repeat(d_skip, HEAD_DIM).reshape(groups, 1, gw))


def _attn_kernel(*refs, dilations):
    nb = len(dilations)
    n = ATTN_WINDOW
    q_refs = refs[0:nb]
    kc_refs, kp_refs = refs[nb:3 * nb:2], refs[nb + 1:3 * nb:2]
    vc_refs, vp_refs = refs[3 * nb:5 * nb:2], refs[3 * nb + 1:5 * nb:2]
    brev_ref, bfar_ref, o_ref = refs[5 * nb:5 * nb + 3]
    scratch = refs[5 * nb + 3:]
    bias_scr, o_scr, l_scr, lg_scr, mx_scr = scratch
    first_tile = pl.program_id(2) == 0

    ri = lax.broadcasted_iota(jnp.int32, (n, n), 0)
    ki = lax.broadcasted_iota(jnp.int32, (n, n), 1)
    low_q = lax.broadcasted_iota(jnp.int32, (n, LANES), 1) < HEAD_DIM
    low_v = lax.broadcasted_iota(jnp.int32, (2 * n, LANES), 1) < HEAD_DIM

    @pl.when(first_tile)
    def _():
        for g in range(nb):
            for e in range(2):
                circ = pltpu.roll(jnp.broadcast_to(brev_ref[g, e:e + 1, :], (n, n)), 0, 1,
                                  stride=1, stride_axis=0)
                far = bfar_ref[g, e:e + 1, :]
                current = jnp.where(ki <= ri, circ, NEG)
                bias_scr[g, e, 0, :, 0:n] = jnp.where(ki > ri, circ,
                                                      jnp.where(ki == ri, far, NEG))
                bias_scr[g, e, 0, :, n:2 * n] = current
                bias_scr[g, e, 1, :, 0:n] = jnp.full((n, n), NEG, F32)
                bias_scr[g, e, 1, :, n:2 * n] = current

    one = jnp.ones((), BF16)
    zero = jnp.zeros((), BF16)
    units = ATTN_TILE // n

    def split(g, idx):
        c, j = divmod(idx, units // dilations[g])
        return c, j, j * n

    def window(cur_ref, prev_ref, c, j):
        prev = prev_ref[c] if j == 0 else cur_ref[c, (j - 1) * n:j * n, :]
        return jnp.concatenate([prev, cur_ref[c, j * n:(j + 1) * n, :]], axis=0)

    def logits_stage(g, idx):
        c, j, start = split(g, idx)
        q_u = q_refs[g][c, start:start + n, :]
        k_w = window(kc_refs[g], kp_refs[g], c, j)
        variant = first_tile.astype(jnp.int32) if j == 0 else 0
        for e in range(2):
            own_q = low_q if e == 0 else ~low_q
            logits = lax.dot_general(jnp.where(own_q, q_u, zero), k_w, NT_DIMS,
                                     preferred_element_type=F32)
            logits = logits + bias_scr[g, e, variant]
            lg_scr[g % 2, idx, e] = logits
            mx_scr[g % 2, idx, e] = jnp.broadcast_to(jnp.max(logits, axis=-1, keepdims=True),
                                                     (n, LANES))

    def value_stage(g, idx):
        r = dilations[g]
        c, j, start = split(g, idx)
        v_w = window(vc_refs[g], vp_refs[g], c, j)
        res = []
        for e in range(2):
            m = mx_scr[g % 2, idx, e]
            p = jnp.concatenate(
                [jnp.exp2(lg_scr[g % 2, idx, e, :, half * n:(half + 1) * n] - m)
                 for half in range(2)], axis=-1).astype(BF16)
            res.append(jnp.dot(p, jnp.where(low_v if e == 0 else ~low_v, v_w, one),
                               preferred_element_type=F32))
        num = jnp.where(low_q, res[0], res[1])
        den = pltpu.roll(jnp.where(low_q, res[1], res[0]), HEAD_DIM, 1)
        rows = pl.ds(j * (n * r) + c, n, stride=r) if r > 1 else pl.ds(start, n)
        o_scr[g, rows, :] = num / den
        l_scr[g, rows, :] = (jnp.where(low_q, mx_scr[g % 2, idx, 0], mx_scr[g % 2, idx, 1])
                             + jnp.log2(den))

    for t in range(nb + 1):
        for idx in range(units):
            if t > 0:
                value_stage(t - 1, idx)
            if t < nb:
                logits_stage(t, idx)

    rb = 2 * n

    def mix(tb, carry):
        rows = pl.ds(pl.multiple_of(tb * rb, rb), rb)
        ls = [l_scr[g, rows, :] for g in range(nb)]
        top = functools.reduce(jnp.maximum, ls)
        ws = [jnp.exp2(l - top) for l in ls]
        acc = functools.reduce(lambda a, b: a + b, [w * o_scr[g, rows, :] for g, w in enumerate(ws)])
        o_ref[rows, :] = (acc / functools.reduce(lambda a, b: a + b, ws)).astype(o_ref.dtype)
        return carry

    lax.fori_loop(0, ATTN_TILE // rb, mix, 0)


def attn_mix(qs, ks, vs, brev, bfar, *, dilations):
    bz, groups, _, s, _ = qs[0].shape
    n, tile = ATTN_WINDOW, ATTN_TILE
    nb = len(dilations)

    def cur(r):
        return pl.BlockSpec((None, None, r, tile // r, LANES), lambda b, hg, i: (b, hg, 0, i, 0))

    def prev(r):
        per_tile = tile // r // n
        return pl.BlockSpec((None, None, r, n, LANES),
                            lambda b, hg, i: (b, hg, 0, jnp.maximum(i * per_tile - 1, 0), 0))

    in_specs = [cur(r) for r in dilations]
    args = list(qs)
    for arrs in (ks, vs):
        for a, r in zip(arrs, dilations):
            in_specs += [cur(r), prev(r)]
            args += [a, a]
    in_specs += [pl.BlockSpec((nb, None, 2, LANES), lambda b, hg, i: (0, hg, 0, 0)),
                 pl.BlockSpec((nb, None, 2, 1), lambda b, hg, i: (0, hg, 0, 0))]
    args += [brev, bfar]
    return pl.pallas_call(
        functools.partial(_attn_kernel, dilations=dilations),
        out_shape=jax.ShapeDtypeStruct((bz, groups, s, LANES), BF16),
        grid=(bz, groups, s // tile),
        in_specs=in_specs,
        out_specs=pl.BlockSpec((None, None, tile, LANES), lambda b, hg, i: (b, hg, i, 0)),
        scratch_shapes=[pltpu.VMEM((nb, 2, 2, n, 2 * n), F32),
                                            pltpu.VMEM((nb, tile, LANES), F32),
                                            pltpu.VMEM((nb, tile, LANES), F32),
                                            pltpu.VMEM((2, tile // n, 2, n, 2 * n), F32),
                                            pltpu.VMEM((2, tile // n, 2, n, LANES), F32)],
        compiler_params=_params(("parallel", "parallel", "arbitrary")),
        name="attn_mix",
    )(*args)


def _t5_causal_bucket(dist):
    max_exact = NUM_BUCKETS // 2
    logv = (jnp.log(jnp.maximum(dist, 1).astype(F32) / max_exact)
            / math.log(MAX_DISTANCE / max_exact))
    large = jnp.minimum(max_exact + (logv * (NUM_BUCKETS - max_exact)).astype(jnp.int32),
                        NUM_BUCKETS - 1)
    return jnp.where(dist < max_exact, dist, large)


def _bias_rows(rel_bias, heads):
    n = ATTN_WINDOW
    rev, far = [], []
    for branch, (window, r) in enumerate(ATTN_BRANCHES):
        buckets = _t5_causal_bucket(jnp.arange(n + 1, dtype=jnp.int32) * r)
        per_dist = rel_bias[buckets][:, branch * heads:(branch + 1) * heads].T.astype(F32) * LOG2_E
        rev.append(per_dist[:, (-jnp.arange(n)) % n])
        far.append(per_dist[:, n:n + 1])
    return (jnp.stack(rev).reshape(len(rev), heads // 2, 2, n),
            jnp.stack(far).reshape(len(far), heads // 2, 2, 1))


def kernel(x, ln_g, ln_b, ffn_w_in, ffn_w_out, m_in_proj, m_conv_w, m_conv_b, m_dt_bias, m_a_log,
           m_d, m_norm_w, m_out_proj, a_w_q, a_w_o, kv_w, rel_bias):
    bz, s, d = x.shape
    t = bz * s
    d_inner = m_out_proj.shape[1]
    conv_dim = m_conv_w.shape[2]
    width = a_w_o.shape[1]
    heads = width // HEAD_DIM
    dilations = tuple(r for _, r in ATTN_BRANCHES)
    nb = len(dilations)
    assert all(window // r == ATTN_WINDOW for window, r in ATTN_BRANCHES)
    x = x.reshape(t, d)
    ffn_w_in, ffn_w_out, m_in_proj, m_out_proj, a_w_q, a_w_o, kv_w = (
        w.astype(BF16) for w in (ffn_w_in, ffn_w_out, m_in_proj, m_out_proj, a_w_q, a_w_o, kv_w))
    ks = vs = brev = bfar = None
    for layer in range(DEPTH):
        x = ffn_ln(x, ffn_w_in, ffn_w_out, ln_g[layer, 0], ln_b[layer, 0], lead=(layer, 0))
        if layer < N_SSD_LAYERS:
            z, xs, bm, cm, dtt = ssd_in(x.reshape(bz, s, d), m_in_proj, m_conv_w[layer],
                                        m_conv_b[layer], m_dt_bias[layer], d_inner=d_inner,
                                        lead=(layer,))
            y = ssd_scan(xs, bm, cm, dtt, m_a_log[layer], m_d[layer])
            mixer = dict(ssd=(y, z.reshape(t, d_inner), m_norm_w[layer], m_out_proj, (layer,),
                              ln_g[layer, 1], ln_b[layer, 1]))
        else:
            a = layer - N_SSD_LAYERS
            qs = proj_cm(x.reshape(bz, s, d), a_w_q, dilations=dilations, width=width,
                         lead=(a,), scale=HEAD_DIM ** -0.5 * LOG2_E)
            o = attn_mix(qs, ks, vs, brev, bfar, dilations=dilations)
            mixer = dict(attn=(o, a_w_o, (a,), ln_g[layer, 1], ln_b[layer, 1]))
        x = ffn_ln(x, ffn_w_in, ffn_w_out, ln_g[layer, 2], ln_b[layer, 2], lead=(layer, 1),
                   **mixer)
        if layer == N_SSD_LAYERS - 1:
            kvs = proj_cm(x.reshape(bz, s, d), kv_w, dilations=dilations * 2, width=width)
            ks, vs = kvs[:nb], kvs[nb:]
            brev, bfar = _bias_rows(rel_bias, heads)
    return x.reshape(bz, s, d)
```

```python
import functools
import math

import jax
import jax.numpy as jnp
from jax import lax
from jax.experimental import pallas as pl
from jax.experimental.pallas import tpu as pltpu

F32 = jnp.float32
BF16 = jnp.bfloat16

DEPTH = 4
N_SSD_LAYERS = DEPTH // 2
ALPHA = (2.0 * DEPTH) ** 0.25
EPS = 1e-5
FFN_RES = 0.5
HEAD_DIM = 64
SSD_GROUPS = 8
SSD_STATE = 128
SSD_CHUNK = 128
SSD_CONV = 4
ATTN_BRANCHES = ((128, 1), (512, 4), (2048, 16))
ATTN_WINDOW = 128
ATTN_TILE = ATTN_WINDOW * max(r for _, r in ATTN_BRANCHES)
NUM_BUCKETS = 32
MAX_DISTANCE = 2048
NEG = -math.inf
LOG2_E = math.log2(math.e)

LANES = 128
SUBLANES = 8
VMEM_LIMIT_BYTES = 56 * 1024 * 1024

NT_DIMS = (((1,), (1,)), ((), ()))


def _params(semantics):
    return pltpu.CompilerParams(dimension_semantics=semantics,
                                vmem_limit_bytes=VMEM_LIMIT_BYTES)


def _resident(shape, lead=()):
    block = (None,) * len(lead) + tuple(shape[len(lead):])
    index = tuple(lead) + (0,) * (len(shape) - len(lead))
    return pl.BlockSpec(block, lambda *_: index, pipeline_mode=pl.Buffered(1))


def _layer_norm(y, g, b):
    mu = jnp.mean(y, axis=-1, keepdims=True)
    d = y - mu
    var = jnp.mean(d * d, axis=-1, keepdims=True)
    return d * lax.rsqrt(var + EPS) * g + b


def _silu(v):
    return v * jax.nn.sigmoid(v)


def _gated_norm_project(y_ref, z_ref, nw_ref, w_ref, rows):
    groups, _, gw = y_ref.shape
    acc = None
    for g in range(groups):
        sl = slice(g * gw, (g + 1) * gw)
        gated = y_ref[g, rows, :].astype(F32) * _silu(z_ref[rows, sl].astype(F32))
        ms = jnp.mean(gated * gated, axis=-1, keepdims=True)
        normed = (gated * lax.rsqrt(ms + EPS) * nw_ref[:, sl]).astype(BF16)
        part = jnp.dot(normed, w_ref[sl, :], preferred_element_type=F32)
        acc = part if acc is None else acc + part
    return acc


def _ffn_kernel(*refs, d_ff, ck, sub, mixer):
    if mixer == "ssd":
        x_ref, y_ref, z_ref, nw_ref, wm_ref, gm_ref, bm_ref = refs[:7]
    elif mixer == "attn":
        x_ref, a_ref, wm_ref, gm_ref, bm_ref = refs[:5]
    else:
        x_ref = refs[0]
    win_ref, wout_ref, g_ref, b_ref, o_ref = refs[-5:]
    for h in range(x_ref.shape[0] // sub):
        rows = slice(h * sub, (h + 1) * sub)
        x = x_ref[rows, :]
        if mixer == "ssd":
            mixed = _gated_norm_project(y_ref, z_ref, nw_ref, wm_ref, rows)
            x = _layer_norm(ALPHA * x + mixed, gm_ref[...], bm_ref[...])
        elif mixer == "attn":
            heads = jnp.concatenate([a_ref[lg, rows, :] for lg in range(a_ref.shape[0])], axis=-1)
            mixed = jnp.dot(heads, wm_ref[...], preferred_element_type=F32)
            x = _layer_norm(ALPHA * x + mixed, gm_ref[...], bm_ref[...])
        xb = x.astype(BF16)
        acc = jnp.zeros(x.shape, F32)
        for c in range(d_ff // ck):
            gate = jnp.dot(xb, win_ref[:, c * ck:(c + 1) * ck], preferred_element_type=F32)
            up = jnp.dot(xb, win_ref[:, d_ff + c * ck:d_ff + (c + 1) * ck],
                         preferred_element_type=F32)
            act = (_silu(gate) * up).astype(BF16)
            acc = acc + jnp.dot(act, wout_ref[c * ck:(c + 1) * ck, :],
                                preferred_element_type=F32)
        o_ref[rows, :] = _layer_norm(ALPHA * x + FFN_RES * acc, g_ref[...], b_ref[...])


def ffn_ln(x, w_in, w_out, g, b, *, lead=(), ssd=None, attn=None, ck=256, sub=512):
    t, d = x.shape
    d_ff = w_out.shape[-2]
    row = lambda v: v.reshape(1, -1)
    if ssd is not None:
        y, z, norm_w, w_proj, proj_lead, gm, bm = ssd
        tm, mixer = sub, "ssd"
        groups, per_batch, gw = y.shape[1], y.shape[2] // tm, y.shape[3]
        pre_specs = [pl.BlockSpec((None, groups, tm, gw),
                                  lambda i: (i // per_batch, 0, i % per_batch, 0)),
                     pl.BlockSpec((tm, z.shape[1]), lambda i: (i, 0)), _resident((1, z.shape[1])),
                     _resident(w_proj.shape, proj_lead), _resident((1, d)), _resident((1, d))]
        pre_args = [y, z, row(norm_w), w_proj, row(gm), row(bm)]
    elif attn is not None:
        heads, w_o, o_lead, gm, bm = attn
        tm, mixer = 2 * sub, "attn"
        groups, per_batch = heads.shape[1], heads.shape[2] // tm
        pre_specs = [pl.BlockSpec((None, groups, tm, LANES),
                                  lambda i: (i // per_batch, 0, i % per_batch, 0)),
                     _resident(w_o.shape, o_lead), _resident((1, d)), _resident((1, d))]
        pre_args = [heads, w_o, row(gm), row(bm)]
    else:
        tm, mixer, pre_specs, pre_args = 2 * sub, None, [], []
    return pl.pallas_call(
        functools.partial(_ffn_kernel, d_ff=d_ff, ck=ck, sub=sub, mixer=mixer),
        out_shape=jax.ShapeDtypeStruct((t, d), F32),
        grid=(t // tm,),
        in_specs=[pl.BlockSpec((tm, d), lambda i: (i, 0))] + pre_specs
                 + [_resident(w_in.shape, lead), _resident(w_out.shape, lead),
                    _resident((1, d)), _resident((1, d))],
        out_specs=pl.BlockSpec((tm, d), lambda i: (i, 0)),
        compiler_params=_params(("parallel",)),
        name="ffn_ln" if mixer is None else f"{mixer}_out_ffn_ln",
    )(x, *pre_args, w_in, w_out, row(g), row(b))


def _proj_cm_kernel(x_ref, w_ref, *refs, dilations, width, scale, sub):
    o_refs, scr = refs[:-1], refs[-1]
    for h in range(x_ref.shape[0] // sub):
        xb = x_ref[h * sub:(h + 1) * sub, :].astype(BF16)
        for n in sorted(range(len(dilations)), key=lambda n: -dilations[n]):
            o_ref, r = o_refs[n], dilations[n]
            out_rows = slice(h * sub // r, (h + 1) * sub // r)
            y = jnp.dot(xb, w_ref[:, n * width:(n + 1) * width],
                        preferred_element_type=F32) * scale
            if r == 1:
                for lg in range(width // LANES):
                    o_ref[lg, 0, out_rows, :] = y[:, lg * LANES:(lg + 1) * LANES].astype(BF16)
            else:
                for lg in range(width // LANES):
                    scr[h, lg] = y[:, lg * LANES:(lg + 1) * LANES]
                for lg in range(width // LANES):
                    for c in range(r):
                        o_ref[lg, c, out_rows, :] = (
                            scr[h, lg, pl.ds(c, sub // r, stride=r), :].astype(BF16))


def proj_cm(x, w, *, dilations, width, lead=(), scale=1.0, tm=1024, sub=1024):
    bz, s, d = x.shape
    groups = width // LANES
    kern = functools.partial(_proj_cm_kernel, dilations=dilations, width=width, scale=scale,
                             sub=sub)
    return pl.pallas_call(
        kern,
        out_shape=tuple(jax.ShapeDtypeStruct((bz, groups, r, s // r, LANES), BF16)
                        for r in dilations),
        grid=(bz, s // tm),
        in_specs=[pl.BlockSpec((None, tm, d), lambda b, i: (b, i, 0)), _resident(w.shape, lead)],
        out_specs=tuple(pl.BlockSpec((None, groups, r, tm // r, LANES),
                                     lambda b, i: (b, 0, 0, i, 0)) for r in dilations),
        scratch_shapes=[pltpu.VMEM((tm // sub, groups, sub, LANES), F32)],
        compiler_params=_params(("parallel", "parallel")),
        name="proj_cm",
    )(x, w)


def _ssd_in_kernel(x_ref, win_ref, wdtt_ref, cw_ref, cb_ref, dtbt_ref,
                   z_ref, xs_ref, bm_ref, cm_ref, dtt_ref, halo_ref, stage_ref,
                   *, tm, cn, d_inner, conv_dim):
    i = pl.program_id(1)
    xs_groups = d_inner // LANES
    per_xs = xs_ref.shape[2] // LANES

    def store_conv(grp, val):
        if grp < xs_groups:
            xs_ref[grp // per_xs, :, (grp % per_xs) * LANES:(grp % per_xs + 1) * LANES] = val
        elif grp < xs_groups + bm_ref.shape[0]:
            bm_ref[grp - xs_groups] = val
        else:
            cm_ref[grp - xs_groups - bm_ref.shape[0]] = val

    xb = x_ref[...].astype(BF16)
    hist = SUBLANES
    per = tm // SUBLANES
    gpc = cn // LANES

    @pl.when(i == 0)
    def _():
        halo_ref[:, 0:hist, :] = jnp.zeros((halo_ref.shape[0], hist, LANES), F32)

    @pl.when(i > 0)
    def _():
        halo_ref[:, 0:hist, :] = halo_ref[:, tm:tm + hist, :]

    for c in range(d_inner // cn):
        sl = slice(c * cn, (c + 1) * cn)
        z_ref[:, sl] = jnp.dot(xb, win_ref[:, sl], preferred_element_type=F32).astype(z_ref.dtype)
    for c in range(conv_dim // cn):
        u = jnp.dot(xb, win_ref[:, d_inner + c * cn:d_inner + (c + 1) * cn],
                    preferred_element_type=F32)
        for lg in range(gpc):
            halo_ref[c * gpc + lg, hist:hist + tm, :] = u[:, lg * LANES:(lg + 1) * LANES]
        for lg in range(gpc):
            grp = c * gpc + lg
            cols = slice(grp * LANES, (grp + 1) * LANES)
            taps = {back: halo_ref[grp, pl.ds(hist - back, per, stride=SUBLANES), :]
                    for back in range(1 - SUBLANES, SSD_CONV)}
            for p in range(SUBLANES):
                conv = cb_ref[:, cols] + jnp.zeros((per, LANES), F32)
                for k in range(SSD_CONV):
                    conv = conv + cw_ref[k:k + 1, cols] * taps[SSD_CONV - 1 - k - p]
                stage_ref[lg, pl.ds(p, per, stride=SUBLANES), :] = _silu(conv)
            store_conv(grp, stage_ref[lg].astype(xs_ref.dtype))
    dtt_raw = lax.dot_general(wdtt_ref[...], xb, NT_DIMS, preferred_element_type=F32)
    dtt_ref[...] = jax.nn.softplus(dtt_raw + dtbt_ref[...])


def ssd_in(x, w_in, conv_w, conv_b, dt_bias, *, d_inner, lead=(), tm=256, cn=512):
    bz, s, d = x.shape
    conv_dim, heads = conv_w.shape[1], dt_bias.shape[0]
    groups = SSD_GROUPS
    gw, n_state = d_inner // groups, (conv_dim - d_inner) // (2 * groups)
    kern = functools.partial(_ssd_in_kernel, tm=tm, cn=cn, d_inner=d_inner, conv_dim=conv_dim)
    grouped = lambda width: pl.BlockSpec((None, groups, tm, width), lambda b, i: (b, 0, i, 0))
    return pl.pallas_call(
        kern,
        out_shape=(jax.ShapeDtypeStruct((bz, s, d_inner), BF16),
                   jax.ShapeDtypeStruct((bz, groups, s, gw), BF16),
                   jax.ShapeDtypeStruct((bz, groups, s, n_state), BF16),
                   jax.ShapeDtypeStruct((bz, groups, s, n_state), BF16),
                   jax.ShapeDtypeStruct((bz, heads, s), F32)),
        grid=(bz, s // tm),
        in_specs=[pl.BlockSpec((None, tm, d), lambda b, i: (b, i, 0)),
                  _resident(w_in.shape, lead), _resident((heads, d)), _resident(conv_w.shape),
                  _resident((1, conv_dim)), _resident((heads, 1))],
        out_specs=(pl.BlockSpec((None, tm, d_inner), lambda b, i: (b, i, 0)),
                   grouped(gw), grouped(n_state), grouped(n_state),
                   pl.BlockSpec((None, heads, tm), lambda b, i: (b, 0, i))),
        scratch_shapes=[pltpu.VMEM((conv_dim // LANES, tm + SUBLANES, LANES), F32),
                        pltpu.VMEM((cn // LANES, tm, LANES), F32)],
        compiler_params=_params(("parallel", "arbitrary")),
        name="ssd_in",
    )(x, w_in, w_in[lead][:, d_inner + conv_dim:].T, conv_w, conv_b.reshape(1, conv_dim),
      dt_bias.reshape(heads, 1))


def _ssd_scan_kernel(xs_ref, b_ref, c_ref, dtt_ref, alog_ref, alogt_ref, dskip_ref, y_ref,
                     h_ref, *, cps):
    ck = SSD_CHUNK
    heads = dtt_ref.shape[0]
    groups, _, gw = h_ref.shape
    hpg = gw // HEAD_DIM
    per_block = LANES // HEAD_DIM

    @pl.when(pl.program_id(1) == 0)
    def _():
        h_ref[...] = jnp.zeros(h_ref.shape, F32)

    a2_row = -jnp.exp(alog_ref[...]) * LOG2_E
    a2_col = -jnp.exp(alogt_ref[...]) * LOG2_E
    ri = lax.broadcasted_iota(jnp.int32, (ck, ck), 0)
    ci = lax.broadcasted_iota(jnp.int32, (ck, ck), 1)
    causal = ci <= ri
    tril = causal.astype(F32)
    triu = (ri <= ci).astype(F32)
    lane = lax.broadcasted_iota(jnp.int32, (ck, gw), 1)
    lane1 = lax.broadcasted_iota(jnp.int32, (1, gw), 1)
    lane_b = lax.broadcasted_iota(jnp.int32, (ck, LANES), 1)
    head_mask = [(lane >= j * HEAD_DIM) & (lane < (j + 1) * HEAD_DIM) for j in range(hpg)]

    def pick_head(vals, lanes):
        out = vals[-1]
        for j in range(len(vals) - 2, -1, -1):
            out = jnp.where(lanes < (j + 1) * HEAD_DIM, vals[j], out)
        return out

    for q in range(cps):
        rows = slice(q * ck, (q + 1) * ck)
        dtt = dtt_ref[:, rows]
        acum = jnp.dot(tril, dtt.T * a2_row, precision=lax.Precision.HIGHEST,
                       preferred_element_type=F32)
        acumt = jnp.dot(dtt * a2_col, triu, precision=lax.Precision.HIGHEST,
                        preferred_element_type=F32)
        a_end = acumt[:, ck - 1:ck]
        wt = jnp.exp2(a_end - acumt) * dtt
        dec = jnp.broadcast_to(jnp.exp2(a_end), (heads, gw))
        for g in range(groups):
            xs_gb = xs_ref[g, rows, :]
            b_gb = b_ref[g, rows, :]
            c_gb = c_ref[g, rows, :]
            b_gt = b_gb.astype(F32).T
            cb = lax.dot_general(c_gb, b_gb, NT_DIMS, preferred_element_type=F32)
            h_g = h_ref[g]
            y_off = jnp.dot(c_gb, h_g.astype(BF16), preferred_element_type=F32)
            xs_heads = jnp.concatenate(
                [jnp.where(head_mask[j], xs_gb, jnp.zeros_like(xs_gb)) for j in range(hpg)], axis=0)
            m_parts, bw_parts, e_blocks = [], [], []
            for j in range(hpg):
                hd = g * hpg + j
                col = jnp.broadcast_to(acum[:, hd:hd + 1], (ck, ck))
                decay = jnp.exp2(jnp.where(causal, col - acumt[hd:hd + 1, :], -jnp.inf))
                m_parts.append((cb * decay * dtt[hd:hd + 1, :]).astype(BF16))
                bw_parts.append((b_gt * wt[hd:hd + 1, :]).astype(BF16))
                e_col = jnp.exp2(col)
                if j % per_block == 0:
                    e_blocks.append(e_col)
                else:
                    e_blocks[-1] = jnp.where(lane_b >= (j % per_block) * HEAD_DIM, e_col,
                                             e_blocks[-1])
            e_start = jnp.concatenate(e_blocks, axis=1)
            y_d = jnp.dot(jnp.concatenate(m_parts, axis=1), xs_heads,
                          preferred_element_type=F32)
            st = jnp.dot(jnp.concatenate(bw_parts, axis=1), xs_heads,
                         preferred_element_type=F32)
            y_g = y_d + y_off * e_start + dskip_ref[g] * xs_gb.astype(F32)
            y_ref[g, rows, :] = y_g.astype(y_ref.dtype)
            dec_g = dec[g * hpg:(g + 1) * hpg, :]
            h_ref[g] = pick_head([dec_g[j:j + 1, :] for j in range(hpg)], lane1) * h_g + st


def ssd_scan(xs, bm, cm, dtt, a_log, d_skip, *, cps=8):
    bz, groups, s, gw = xs.shape
    n_state = bm.shape[3]
    heads = a_log.shape[0]
    lt = cps * SSD_CHUNK
    seq = lambda width: pl.BlockSpec((None, groups, lt, width), lambda b, i: (b, 0, i, 0))
    return pl.pallas_call(
        functools.partial(_ssd_scan_kernel, cps=cps),
        out_shape=jax.ShapeDtypeStruct((bz, groups, s, gw), BF16),
        grid=(bz, s // lt),
        in_specs=[seq(gw), seq(n_state), seq(n_state),
                  pl.BlockSpec((None, heads, lt), lambda b, i: (b, 0, i)),
                  _resident((1, heads)), _resident((heads, 1)), _resident((groups, 1, gw))],
        out_specs=seq(gw),
        scratch_shapes=[pltpu.VMEM((groups, n_state, gw), F32)],
        compiler_params=_params(("parallel", "arbitrary")),
        name="ssd_scan",
    )(xs, bm, cm, dtt, a_log.reshape(1, heads), a_log.reshape(heads, 1),
      jnp.repeat(d_skip, HEAD_DIM).reshape(groups, 1, gw))


def _attn_kernel(*refs, dilations):
    nb = len(dilations)
    n = ATTN_WINDOW
    q_refs = refs[0:nb]
    kc_refs, kp_refs = refs[nb:3 * nb:2], refs[nb + 1:3 * nb:2]
    vc_refs, vp_refs = refs[3 * nb:5 * nb:2], refs[3 * nb + 1:5 * nb:2]
    brev_ref, bfar_ref, o_ref = refs[5 * nb:5 * nb + 3]
    scratch = refs[5 * nb + 3:]
    bias_scr, o_scr, l_scr, lg_scr, mx_scr = scratch
    first_tile = pl.program_id(2) == 0

    ri = lax.broadcasted_iota(jnp.int32, (n, n), 0)
    ki = lax.broadcasted_iota(jnp.int32, (n, n), 1)
    low_q = lax.broadcasted_iota(jnp.int32, (n, LANES), 1) < HEAD_DIM
    low_v = lax.broadcasted_iota(jnp.int32, (2 * n, LANES), 1) < HEAD_DIM

    @pl.when(first_tile)
    def _():
        for g in range(nb):
            for e in range(2):
                circ = pltpu.roll(jnp.broadcast_to(brev_ref[g, e:e + 1, :], (n, n)), 0, 1,
                                  stride=1, stride_axis=0)
                far = bfar_ref[g, e:e + 1, :]
                current = jnp.where(ki <= ri, circ, NEG)
                bias_scr[g, e, 0, :, 0:n] = jnp.where(ki > ri, circ,
                                                      jnp.where(ki == ri, far, NEG))
                bias_scr[g, e, 0, :, n:2 * n] = current
                bias_scr[g, e, 1, :, 0:n] = jnp.full((n, n), NEG, F32)
                bias_scr[g, e, 1, :, n:2 * n] = current

    one = jnp.ones((), BF16)
    zero = jnp.zeros((), BF16)
    units = ATTN_TILE // n

    def split(g, idx):
        c, j = divmod(idx, units // dilations[g])
        return c, j, j * n

    def window(cur_ref, prev_ref, c, j):
        prev = prev_ref[c] if j == 0 else cur_ref[c, (j - 1) * n:j * n, :]
        return jnp.concatenate([prev, cur_ref[c, j * n:(j + 1) * n, :]], axis=0)

    def logits_stage(g, idx):
        c, j, start = split(g, idx)
        q_u = q_refs[g][c, start:start + n, :]
        k_w = window(kc_refs[g], kp_refs[g], c, j)
        variant = first_tile.astype(jnp.int32) if j == 0 else 0
        for e in range(2):
            own_q = low_q if e == 0 else ~low_q
            logits = lax.dot_general(jnp.where(own_q, q_u, zero), k_w, NT_DIMS,
                                     preferred_element_type=F32)
            logits = logits + bias_scr[g, e, variant]
            lg_scr[g % 2, idx, e] = logits
            mx_scr[g % 2, idx, e] = jnp.broadcast_to(jnp.max(logits, axis=-1, keepdims=True),
                                                     (n, LANES))

    def value_stage(g, idx):
        r = dilations[g]
        c, j, start = split(g, idx)
        v_w = window(vc_refs[g], vp_refs[g], c, j)
        res = []
        for e in range(2):
            m = mx_scr[g % 2, idx, e]
            p = jnp.concatenate(
                [jnp.exp2(lg_scr[g % 2, idx, e, :, half * n:(half + 1) * n] - m)
                 for half in range(2)], axis=-1).astype(BF16)
            res.append(jnp.dot(p, jnp.where(low_v if e == 0 else ~low_v, v_w, one),
                               preferred_element_type=F32))
        num = jnp.where(low_q, res[0], res[1])
        den = pltpu.roll(jnp.where(low_q, res[1], res[0]), HEAD_DIM, 1)
        rows = pl.ds(j * (n * r) + c, n, stride=r) if r > 1 else pl.ds(start, n)
        o_scr[g, rows, :] = num / den
        l_scr[g, rows, :] = (jnp.where(low_q, mx_scr[g % 2, idx, 0], mx_scr[g % 2, idx, 1])
                             + jnp.log2(den))

    for t in range(nb + 1):
        for idx in range(units):
            if t > 0:
                value_stage(t - 1, idx)
            if t < nb:
                logits_stage(t, idx)

    rb = 2 * n

    def mix(tb, carry):
        rows = pl.ds(pl.multiple_of(tb * rb, rb), rb)
        ls = [l_scr[g, rows, :] for g in range(nb)]
        top = functools.reduce(jnp.maximum, ls)
        ws = [jnp.exp2(l - top) for l in ls]
        acc = functools.reduce(lambda a, b: a + b, [w * o_scr[g, rows, :] for g, w in enumerate(ws)])
        o_ref[rows, :] = (acc / functools.reduce(lambda a, b: a + b, ws)).astype(o_ref.dtype)
        return carry

    lax.fori_loop(0, ATTN_TILE // rb, mix, 0)


def attn_mix(qs, ks, vs, brev, bfar, *, dilations):
    bz, groups, _, s, _ = qs[0].shape
    n, tile = ATTN_WINDOW, ATTN_TILE
    nb = len(dilations)

    def cur(r):
        return pl.BlockSpec((None, None, r, tile // r, LANES), lambda b, hg, i: (b, hg, 0, i, 0))

    def prev(r):
        per_tile = tile // r // n
        return pl.BlockSpec((None, None, r, n, LANES),
                            lambda b, hg, i: (b, hg, 0, jnp.maximum(i * per_tile - 1, 0), 0))

    in_specs = [cur(r) for r in dilations]
    args = list(qs)
    for arrs in (ks, vs):
        for a, r in zip(arrs, dilations):
            in_specs += [cur(r), prev(r)]
            args += [a, a]
    in_specs += [pl.BlockSpec((nb, None, 2, LANES), lambda b, hg, i: (0, hg, 0, 0)),
                 pl.BlockSpec((nb, None, 2, 1), lambda b, hg, i: (0, hg, 0, 0))]
    args += [brev, bfar]
    return pl.pallas_call(
        functools.partial(_attn_kernel, dilations=dilations),
        out_shape=jax.ShapeDtypeStruct((bz, groups, s, LANES), BF16),
        grid=(bz, groups, s // tile),
        in_specs=in_specs,
        out_specs=pl.BlockSpec((None, None, tile, LANES), lambda b, hg, i: (b, hg, i, 0)),
        scratch_shapes=[pltpu.VMEM((nb, 2, 2, n, 2 * n), F32),
                                            pltpu.VMEM((nb, tile, LANES), F32),
                                            pltpu.VMEM((nb, tile, LANES), F32),
                                            pltpu.VMEM((2, tile // n, 2, n, 2 * n), F32),
                                            pltpu.VMEM((2, tile // n, 2, n, LANES), F32)],
        compiler_params=_params(("parallel", "parallel", "arbitrary")),
        name="attn_mix",
    )(*args)


def _t5_causal_bucket(dist):
    max_exact = NUM_BUCKETS // 2
    logv = (jnp.log(jnp.maximum(dist, 1).astype(F32) / max_exact)
            / math.log(MAX_DISTANCE / max_exact))
    large = jnp.minimum(max_exact + (logv * (NUM_BUCKETS - max_exact)).astype(jnp.int32),
                        NUM_BUCKETS - 1)
    return jnp.where(dist < max_exact, dist, large)


def _bias_rows(rel_bias, heads):
    n = ATTN_WINDOW
    rev, far = [], []
    for branch, (window, r) in enumerate(ATTN_BRANCHES):
        buckets = _t5_causal_bucket(jnp.arange(n + 1, dtype=jnp.int32) * r)
        per_dist = rel_bias[buckets][:, branch * heads:(branch + 1) * heads].T.astype(F32) * LOG2_E
        rev.append(per_dist[:, (-jnp.arange(n)) % n])
        far.append(per_dist[:, n:n + 1])
    return (jnp.stack(rev).reshape(len(rev), heads // 2, 2, n),
            jnp.stack(far).reshape(len(far), heads // 2, 2, 1))


def kernel(x, ln_g, ln_b, ffn_w_in, ffn_w_out, m_in_proj, m_conv_w, m_conv_b, m_dt_bias, m_a_log,
           m_d, m_norm_w, m_out_proj, a_w_q, a_w_o, kv_w, rel_bias):
    bz, s, d = x.shape
    t = bz * s
    d_inner = m_out_proj.shape[1]
    conv_dim = m_conv_w.shape[2]
    width = a_w_o.shape[1]
    heads = width // HEAD_DIM
    dilations = tuple(r for _, r in ATTN_BRANCHES)
    nb = len(dilations)
    assert all(window // r == ATTN_WINDOW for window, r in ATTN_BRANCHES)
    x = x.reshape(t, d)
    ffn_w_in, ffn_w_out, m_in_proj, m_out_proj, a_w_q, a_w_o, kv_w = (
        w.astype(BF16) for w in (ffn_w_in, ffn_w_out, m_in_proj, m_out_proj, a_w_q, a_w_o, kv_w))
    ks = vs = brev = bfar = None
    for layer in range(DEPTH):
        x = ffn_ln(x, ffn_w_in, ffn_w_out, ln_g[layer, 0], ln_b[layer, 0], lead=(layer, 0))
        if layer < N_SSD_LAYERS:
            z, xs, bm, cm, dtt = ssd_in(x.reshape(bz, s, d), m_in_proj, m_conv_w[layer],
                                        m_conv_b[layer], m_dt_bias[layer], d_inner=d_inner,
                                        lead=(layer,))
            y = ssd_scan(xs, bm, cm, dtt, m_a_log[layer], m_d[layer])
            mixer = dict(ssd=(y, z.reshape(t, d_inner), m_norm_w[layer], m_out_proj, (layer,),
                              ln_g[layer, 1], ln_b[layer, 1]))
        else:
            a = layer - N_SSD_LAYERS
            qs = proj_cm(x.reshape(bz, s, d), a_w_q, dilations=dilations, width=width,
                         lead=(a,), scale=HEAD_DIM ** -0.5 * LOG2_E)
            o = attn_mix(qs, ks, vs, brev, bfar, dilations=dilations)
            mixer = dict(attn=(o, a_w_o, (a,), ln_g[layer, 1], ln_b[layer, 1]))
        x = ffn_ln(x, ffn_w_in, ffn_w_out, ln_g[layer, 2], ln_b[layer, 2], lead=(layer, 1),
                   **mixer)
        if layer == N_SSD_LAYERS - 1:
            kvs = proj_cm(x.reshape(bz, s, d), kv_w, dilations=dilations * 2, width=width)
            ks, vs = kvs[:nb], kvs[nb:]
            brev, bfar = _bias_rows(rel_bias, heads)
    return x.reshape(bz, s, d)
```
